```python
import jax, jax.numpy as jnp
from jax import lax
import numpy as np

D_MODEL = 4096
BATCH = 32
SEQ = 256
DEPTH = 2
DEC_BATCH = 4
DEC_SEQ = 1024
PAST_LEN = 256

GRID_W = 64
CONV_W = 2048
CONV_K = 3
N_HEADS = 16
N_KV = 4
HEAD_DIM = 128
ATT_W = N_HEADS * HEAD_DIM
Q_BLOCK = 128
ROPE_THETA = 10000.0
GLA_HEADS = 4
GLA_DK = 256
GLA_DV = 512
GLA_KW = GLA_HEADS * GLA_DK
GLA_VW = GLA_HEADS * GLA_DV
GLA_RANK = 16
GLA_TAU = 16.0
GLA_CHUNK = 64
N_EXPERTS = 16
EXPERT_FF = D_MODEL // 2
EC_FACTOR = 2
EPS = 1e-6

SPLITS = (CONV_W, CONV_W, CONV_W,
          ATT_W, N_KV * HEAD_DIM, N_KV * HEAD_DIM,
          GLA_KW, GLA_KW, GLA_VW, GLA_VW, 2 * GLA_RANK,
          D_MODEL, D_MODEL, D_MODEL)
IN_COLS = sum(SPLITS)

kernel_name = 'hybrid_diffusion_ctxprefix_step'

F32 = jnp.float32


def rms_norm(x, gain):
    xf = x.astype(F32)
    y = xf * lax.rsqrt(jnp.mean(xf * xf, axis=-1, keepdims=True) + EPS)
    return (y * gain.astype(F32)).astype(x.dtype)


def modulation(cond, w_mod, b_mod):
    m = jax.nn.silu(cond) @ w_mod + b_mod
    return jnp.split(m[:, None, :], 6, axis=-1)


def short_conv(gb, gc, hx, w):
    u = gc * hx
    up = jnp.pad(u, ((0, 0), (1, 1), (0, 0)))
    z = w[0] * up[:, :-2] + w[1] * up[:, 1:-1] + w[2] * up[:, 2:]
    return gb * z


def grid_positions(n):
    rows = n // GRID_W
    r = jnp.repeat(jnp.arange(rows, dtype=jnp.int32), GRID_W)
    col = jnp.arange(rows * GRID_W, dtype=jnp.int32) % GRID_W
    return r, col


def rope_axis(x, pos):
    half = x.shape[-1] // 2
    inv = ROPE_THETA ** (-jnp.arange(half, dtype=F32) / half)
    ang = pos.astype(F32)[:, None] * inv[None, :]
    cos = jnp.cos(ang)[None, :, None, :]
    sin = jnp.sin(ang)[None, :, None, :]
    xf = x.astype(F32)
    x1, x2 = xf[..., :half], xf[..., half:]
    return jnp.concatenate([x1 * cos - x2 * sin, x2 * cos + x1 * sin], axis=-1).astype(x.dtype)


def rope_2d(x, rpos, cpos):
    h = x.shape[-1] // 2
    return jnp.concatenate([rope_axis(x[..., :h], rpos), rope_axis(x[..., h:], cpos)], axis=-1)


def block_attention(q, k, v):
    B, nq = q.shape[:2]
    nb = nq // Q_BLOCK
    qb = jnp.swapaxes(q.reshape((B, nb, Q_BLOCK) + q.shape[2:]), 0, 1)
    scale = HEAD_DIM ** -0.5

    def one_block(qi):
        s = jnp.einsum('bqkgd,bskd->bkgqs', qi, k, preferred_element_type=F32) * scale
        p = jax.nn.softmax(s, axis=-1).astype(v.dtype)
        return jnp.einsum('bkgqs,bskd->bqkgd', p, v)

    o = lax.map(one_block, qb)
    return jnp.swapaxes(o, 0, 1).reshape(q.shape)


def gla_chunked(q, k, v, logg, s0):
    B, n, H, dk = q.shape
    dv = v.shape[-1]
    L = GLA_CHUNK
    nc = n // L

    def to_chunks(t):
        return t.astype(F32).reshape(B, nc, L, H, t.shape[-1]).transpose(1, 0, 3, 2, 4)

    mask = jnp.tril(jnp.ones((L, L), dtype=bool))

    def step(S, inp):
        qc, kc, vc, gc = inp
        b = jnp.cumsum(gc, axis=-2)
        b_last = b[:, :, -1:, :]
        qe = qc * jnp.exp(b)
        ke = kc * jnp.exp(-b)
        A = jnp.where(mask, jnp.einsum('bhtd,bhsd->bhts', qe, ke), 0.0)
        o = jnp.einsum('bhtd,bhde->bhte', qe, S) + jnp.einsum('bhts,bhse->bhte', A, vc)
        S = jnp.exp(b_last[:, :, 0, :])[..., None] * S + jnp.einsum('bhsd,bhse->bhde', kc * jnp.exp(b_last - b), vc)
        return S, o

    S, o = lax.scan(step, s0.astype(F32), (to_chunks(q), to_chunks(k), to_chunks(v), to_chunks(logg)))
    o = o.transpose(1, 0, 3, 2, 4).reshape(B, n, H, dv)
    return o, S


def gla_bidirectional(q, k, v, g_f, g_b, s_f0, s_b0):
    flip = lambda t: t[:, ::-1]
    o_f, s_f = gla_chunked(q, k, v, g_f, s_f0)
    o_b, s_b = gla_chunked(flip(q), flip(k), flip(v), flip(g_b), s_b0)
    return o_f + flip(o_b), s_f, s_b


def mixer_block(h, lp, pos, ctx):
    B, n, _ = h.shape
    offsets = np.cumsum(SPLITS)[:-1].tolist()
    (cb, cc, ch, aq, ak, av, gq, gk, gv, gr, glr,
     m_conv, m_att, m_gla) = jnp.split(h @ lp['w_in'], offsets, axis=-1)

    y_conv = short_conv(cb, cc, ch, lp['conv_w'])

    q = rms_norm(aq.reshape(B, n, N_HEADS, HEAD_DIM), lp['q_norm'])
    k = rms_norm(ak.reshape(B, n, N_KV, HEAD_DIM), lp['k_norm'])
    v = av.reshape(B, n, N_KV, HEAD_DIM)
    if ctx is None:
        keys, vals = k, v
    else:
        rpos, cpos = pos
        q = rope_2d(q, rpos, cpos)
        keys = jnp.concatenate([rope_2d(k, rpos, cpos), ctx[0].astype(k.dtype)], axis=1)
        vals = jnp.concatenate([v, ctx[1].astype(v.dtype)], axis=1)
    q = q.reshape(B, n, N_KV, N_HEADS // N_KV, HEAD_DIM)
    y_att = block_attention(q, keys, vals).reshape(B, n, ATT_W)

    qg = gq.reshape(B, n, GLA_HEADS, GLA_DK) * (GLA_DK ** -0.5)
    kg = gk.reshape(B, n, GLA_HEADS, GLA_DK)
    vg = gv.reshape(B, n, GLA_HEADS, GLA_DV)
    lr_f, lr_b = jnp.split(glr, 2, axis=-1)
    g_f = (jax.nn.log_sigmoid((lr_f @ lp['gla_w2'][0] + lp['gla_b'][0]).astype(F32)) / GLA_TAU).reshape(B, n, GLA_HEADS, GLA_DK)
    g_b = (jax.nn.log_sigmoid((lr_b @ lp['gla_w2'][1] + lp['gla_b'][1]).astype(F32)) / GLA_TAU).reshape(B, n, GLA_HEADS, GLA_DK)
    if ctx is None:
        s_f0 = jnp.zeros((B, GLA_HEADS, GLA_DK, GLA_DV), F32)
        s_b0 = s_f0
    else:
        s_f0, s_b0 = ctx[2], ctx[3]
    o_gla, s_f, s_b = gla_bidirectional(qg, kg, vg, g_f, g_b, s_f0, s_b0)
    o_gla = rms_norm(o_gla, lp['gla_norm']).reshape(B, n, GLA_VW).astype(h.dtype)
    y_gla = o_gla * jax.nn.silu(gr)

    merged = (jax.nn.sigmoid(m_conv) * (y_conv @ lp['w_conv_out'])
              + jax.nn.sigmoid(m_att) * (y_att @ lp['w_att_out'])
              + jax.nn.sigmoid(m_gla) * (y_gla @ lp['w_gla_out']))
    out = merged @ lp['w_o']
    side = (k, v, s_f.astype(h.dtype), s_b.astype(h.dtype)) if ctx is None else None
    return out, side


def expert_choice_ffn(x, w_router, w_gate, w_up, w_down):
    B, n, D = x.shape
    cap = EC_FACTOR * n // N_EXPERTS
    aff = jax.nn.softmax((x @ w_router).astype(F32), axis=-1)
    g, idx = lax.top_k(jnp.swapaxes(aff, 1, 2), cap)
    xs = jax.vmap(lambda xb, ib: xb[ib])(x, idx)
    hid = jax.nn.silu(jnp.einsum('becd,edf->becf', xs, w_gate)) * jnp.einsum('becd,edf->becf', xs, w_up)
    ye = jnp.einsum('becf,efd->becd', hid, w_down) * g[..., None].astype(x.dtype)
    return jax.vmap(lambda yb, ib: jnp.zeros((n, D), x.dtype).at[ib.reshape(-1)].add(yb.reshape(-1, D)))(ye, idx)


def trunk_layer(x, cond, lp, pos, ctx):
    sh1, sc1, ga1, sh2, sc2, ga2 = modulation(cond, lp['w_mod'], lp['b_mod'])
    h = rms_norm(x, lp['norm1']) * (1 + sc1) + sh1
    mix, side = mixer_block(h, lp, pos, ctx)
    x = x + ga1 * mix
    h = rms_norm(x, lp['norm2']) * (1 + sc2) + sh2
    x = x + ga2 * expert_choice_ffn(h, lp['w_router'], lp['w_gate'], lp['w_up'], lp['w_down'])
    return x, side


def setup_inputs(seed: int = 0) -> dict:
    key = jax.random.key(seed)
    ks = jax.random.split(key, 32)

    def nrm(k, shape, scale=1.0):
        return jax.random.normal(k, shape, F32) * scale

    return {
        'x_prompt': nrm(ks[0], (BATCH, SEQ, D_MODEL)),
        'x_sample': nrm(ks[1], (DEC_BATCH, DEC_SEQ, D_MODEL)),
        'cache_k': nrm(ks[2], (DEC_BATCH, DEPTH, PAST_LEN, N_KV, HEAD_DIM)),
        'cache_v': nrm(ks[3], (DEC_BATCH, DEPTH, PAST_LEN, N_KV, HEAD_DIM)),
        'state_gla_fwd': nrm(ks[4], (DEC_BATCH, DEPTH, GLA_HEADS, GLA_DK, GLA_DV)),
        'state_gla_bwd': nrm(ks[5], (DEC_BATCH, DEPTH, GLA_HEADS, GLA_DK, GLA_DV)),
        'c': nrm(ks[6], (DEC_BATCH, D_MODEL)),
        'c_ctx': nrm(ks[7], (D_MODEL,)),
        'w_mod': nrm(ks[8], (DEPTH, D_MODEL, 6 * D_MODEL), 0.5 * D_MODEL ** -0.5),
        'b_mod': nrm(ks[9], (DEPTH, 6 * D_MODEL), 0.02),
        'norm1': 1.0 + nrm(ks[10], (DEPTH, D_MODEL), 0.02),
        'norm2': 1.0 + nrm(ks[11], (DEPTH, D_MODEL), 0.02),
        'w_in': nrm(ks[12], (DEPTH, D_MODEL, IN_COLS), D_MODEL ** -0.5),
        'conv_w': nrm(ks[13], (DEPTH, CONV_K, CONV_W), CONV_K ** -0.5),
        'q_norm': 1.0 + nrm(ks[14], (DEPTH, HEAD_DIM), 0.02),
        'k_norm': 1.0 + nrm(ks[15], (DEPTH, HEAD_DIM), 0.02),
        'gla_w2': nrm(ks[16], (DEPTH, 2, GLA_RANK, GLA_KW), GLA_RANK ** -0.5),
        'gla_b': nrm(ks[17], (DEPTH, 2, GLA_KW), 0.1),
        'gla_norm': 1.0 + nrm(ks[18], (DEPTH, GLA_DV), 0.02),
        'w_conv_out': nrm(ks[19], (DEPTH, CONV_W, D_MODEL), CONV_W ** -0.5),
        'w_att_out': nrm(ks[20], (DEPTH, ATT_W, D_MODEL), ATT_W ** -0.5),
        'w_gla_out': nrm(ks[21], (DEPTH, GLA_VW, D_MODEL), GLA_VW ** -0.5),
        'w_o': nrm(ks[22], (DEPTH, D_MODEL, D_MODEL), D_MODEL ** -0.5),
        'w_router': nrm(ks[23], (DEPTH, D_MODEL, N_EXPERTS), D_MODEL ** -0.5),
        'w_gate': nrm(ks[24], (DEPTH, N_EXPERTS, D_MODEL, EXPERT_FF), D_MODEL ** -0.5),
        'w_up': nrm(ks[25], (DEPTH, N_EXPERTS, D_MODEL, EXPERT_FF), D_MODEL ** -0.5),
        'w_down': nrm(ks[26], (DEPTH, N_EXPERTS, EXPERT_FF, D_MODEL), EXPERT_FF ** -0.5),
    }


def reference(x_prompt, x_sample, cache_k, cache_v, state_gla_fwd, state_gla_bwd, c, c_ctx,
              w_mod, b_mod, norm1, norm2, w_in, conv_w, q_norm, k_norm, gla_w2, gla_b, gla_norm,
              w_conv_out, w_att_out, w_gla_out, w_o, w_router, w_gate, w_up, w_down):
    pos = grid_positions(x_sample.shape[1])
    y_prompt = x_prompt
    y_sample = x_sample
    new_k, new_v, new_sf, new_sb = [], [], [], []
    for l in range(DEPTH):
        lp = dict(w_mod=w_mod[l], b_mod=b_mod[l], norm1=norm1[l], norm2=norm2[l], w_in=w_in[l],
                  conv_w=conv_w[l], q_norm=q_norm[l], k_norm=k_norm[l], gla_w2=gla_w2[l],
                  gla_b=gla_b[l], gla_norm=gla_norm[l], w_conv_out=w_conv_out[l],
                  w_att_out=w_att_out[l], w_gla_out=w_gla_out[l], w_o=w_o[l],
                  w_router=w_router[l], w_gate=w_gate[l], w_up=w_up[l], w_down=w_down[l])
        y_prompt, side = trunk_layer(y_prompt, c_ctx[None, :], lp, None, None)
        new_k.append(side[0])
        new_v.append(side[1])
        new_sf.append(side[2])
        new_sb.append(side[3])
        ctx = (cache_k[:, l], cache_v[:, l], state_gla_fwd[:, l], state_gla_bwd[:, l])
        y_sample, _ = trunk_layer(y_sample, c, lp, pos, ctx)
    new_cache_k = jnp.stack(new_k, axis=1)
    new_cache_v = jnp.stack(new_v, axis=1)
    new_state_gla_fwd = jnp.stack(new_sf, axis=1)
    new_state_gla_bwd = jnp.stack(new_sb, axis=1)
    return (y_prompt, y_sample, new_cache_k, new_cache_v, new_state_gla_fwd, new_state_gla_bwd)
```

```python
import functools

import jax
import jax.numpy as jnp
import numpy as np
from jax import lax
from jax.experimental import pallas as pl
from jax.experimental.pallas import tpu as pltpu

F32 = jnp.float32
BF16 = jnp.bfloat16
HIGHEST = lax.Precision.HIGHEST

GRID_W = 64
GLA_CHUNK = 64
GLA_TAU = 16.0
EC_FACTOR = 2
ROPE_THETA = 10000.0
EPS = 1e-6

LANES = 128
MOD_BLK = 256
NT_DIMS = (((1,), (1,)), ((), ()))
TN_DIMS = (((0,), (0,)), ((), ()))


def _params(sem, vmem_mb):
    return pltpu.CompilerParams(dimension_semantics=sem, vmem_limit_bytes=vmem_mb << 20)


def _any_spec():
    return pl.BlockSpec(memory_space=pl.ANY)


def _sigmoid(x):
    return 1.0 / (1.0 + jnp.exp(-x))


def _silu(x):
    return x * _sigmoid(x)


def _pick(total, candidates):
    for c in candidates:
        if total % c == 0:
            return c
    raise ValueError(f"no tile in {candidates} divides {total}")


def _mod_kernel(c_ref, w_ref, b_ref, o_ref):
    s = _silu(c_ref[...]).astype(BF16)
    o_ref[...] = jnp.dot(s, w_ref[...].astype(BF16), preferred_element_type=F32) + b_ref[...]


def _modulation(cond8, w_mod, b_mod):
    depth, d, n6 = w_mod.shape
    tn = _pick(n6, (512, 256, 128))
    return pl.pallas_call(
        _mod_kernel,
        out_shape=jax.ShapeDtypeStruct((depth, 8, n6), F32),
        grid=(depth, n6 // tn),
        in_specs=[pl.BlockSpec((8, d), lambda l, j: (0, 0)),
                  pl.BlockSpec((None, d, tn), lambda l, j: (l, 0, j)),
                  pl.BlockSpec((None, 1, tn), lambda l, j: (l, 0, j))],
        out_specs=pl.BlockSpec((None, 8, tn), lambda l, j: (l, 0, j)),
        compiler_params=_params(("arbitrary", "arbitrary"), 40),
        name="modulation",
    )(cond8, w_mod, b_mod.reshape(depth, 1, n6))


def _normmod_kernel(x_ref, m_ref, g_ref, *rest, d, sh_off, sc_off, router):
    x = x_ref[...]
    y = x * lax.rsqrt(jnp.mean(x * x, axis=-1, keepdims=True) + EPS) * g_ref[...]
    h = y * (1.0 + m_ref[:, sc_off:sc_off + d]) + m_ref[:, sh_off:sh_off + d]
    if router:
        wr_ref, o_ref, lg_ref = rest
        lg_ref[...] = jnp.dot(h, wr_ref[...], preferred_element_type=F32, precision=HIGHEST)
    else:
        (o_ref,) = rest
    o_ref[...] = h.astype(BF16)


def _normmod(x, modrows, gain, sh_off, sc_off, w_router_pad=None):
    t, d = x.shape
    router = w_router_pad is not None
    in_specs = [pl.BlockSpec((MOD_BLK, d), lambda i: (i, 0)),
                pl.BlockSpec((None, 1, 6 * d), lambda i: (i, 0, 0)),
                pl.BlockSpec((1, d), lambda i: (0, 0))]
    args = [x, modrows, gain.reshape(1, d)]
    out_shape = [jax.ShapeDtypeStruct((t, d), BF16)]
    out_specs = [pl.BlockSpec((MOD_BLK, d), lambda i: (i, 0))]
    if router:
        in_specs.append(pl.BlockSpec((d, LANES), lambda i: (0, 0)))
        args.append(w_router_pad)
        out_shape.append(jax.ShapeDtypeStruct((t, LANES), F32))
        out_specs.append(pl.BlockSpec((MOD_BLK, LANES), lambda i: (i, 0)))
    out = pl.pallas_call(
        functools.partial(_normmod_kernel, d=d, sh_off=sh_off, sc_off=sc_off, router=router),
        out_shape=out_shape, grid=(t // MOD_BLK,), in_specs=in_specs, out_specs=out_specs,
        compiler_params=_params(("arbitrary",), 40),
        name="normmod_router" if router else "normmod",
    )(*args)
    return out if router else out[0]


def _mm_kernel(*refs, n_a, n_w, n_sides, inner_axis, epilogue):
    a_refs = refs[:n_a]
    w_refs = refs[n_a:n_a + n_w]
    side_refs = refs[n_a + n_w:n_a + n_w + n_sides]
    o_ref = refs[n_a + n_w + n_sides]
    wb_refs = refs[n_a + n_w + n_sides + 1:]

    @pl.when(pl.program_id(inner_axis) == 0)
    def _cast():
        for w, wb in zip(w_refs, wb_refs):
            wb[...] = w[...].astype(BF16)

    accs = [jnp.dot(a_refs[q if n_a > 1 else 0][...], wb[...], preferred_element_type=F32)
            for q, wb in enumerate(wb_refs)]
    o_ref[...] = epilogue(accs, [s[...] for s in side_refs]).astype(o_ref.dtype)


def _fused_mm(grid, a_args, w_args, side_args, out_shape, out_spec, epilogue, wb_shapes, vmem_mb, name):
    assert len(a_args) in (1, len(w_args))
    arrays = [a for a, _ in a_args] + [w for w, _ in w_args] + [s for s, _ in side_args]
    specs = [s for _, s in a_args] + [s for _, s in w_args] + [s for _, s in side_args]
    body = functools.partial(_mm_kernel, n_a=len(a_args), n_w=len(w_args), n_sides=len(side_args),
                             inner_axis=len(grid) - 1, epilogue=epilogue)
    return pl.pallas_call(
        body, out_shape=out_shape, grid=grid, in_specs=specs, out_specs=out_spec,
        scratch_shapes=[pltpu.VMEM(s, BF16) for s in wb_shapes],
        compiler_params=_params(("arbitrary",) * len(grid), vmem_mb), name=name,
    )(*arrays)


def _epi_identity(accs, sides):
    return accs[0]


def _epi_sigmoid(accs, sides):
    return _sigmoid(accs[0])


def _epi_residual(accs, sides):
    x, ga = sides
    return x + ga * accs[0]


def _epi_swiglu(accs, sides):
    return _silu(accs[0]) * accs[1]


def _epi_merge(accs, sides):
    return sides[0] * accs[0] + sides[1] * accs[1] + sides[2] * accs[2]


def _epi_rowscale(accs, sides):
    return accs[0] * sides[0]


def _proj(h, w3, l, n_cols, out_dtype, epilogue, name, col0_blocks=0, tn=512):
    t, k = h.shape
    tm = _pick(t, (1024, 512, 256))
    return _fused_mm(
        (n_cols // tn, t // tm),
        [(h, pl.BlockSpec((tm, k), lambda j, i: (i, 0)))],
        [(w3, pl.BlockSpec((None, k, tn), lambda j, i: (l, 0, j + col0_blocks)))],
        [], jax.ShapeDtypeStruct((t, n_cols), out_dtype),
        pl.BlockSpec((tm, tn), lambda j, i: (i, j)), epilogue, [(k, tn)], 48, name)


def _merge(y_conv, y_att, y_gla, gates, w_conv_out, w_att_out, w_gla_out, l, d):
    t = y_conv.shape[0]
    tm = _pick(t, (512, 256))
    tn = _pick(d, (256, 128))
    nb = d // tn
    ys = (y_conv, y_att, y_gla)
    ws = (w_conv_out, w_att_out, w_gla_out)
    a_args = [(y, pl.BlockSpec((tm, y.shape[1]), lambda j, i: (i, 0))) for y in ys]
    w_args = [(w, pl.BlockSpec((None, w.shape[1], tn), lambda j, i: (l, 0, j))) for w in ws]
    side_args = [(gates, pl.BlockSpec((tm, tn), functools.partial(lambda j, i, q: (i, q * nb + j), q=q)))
                 for q in range(3)]
    return _fused_mm((nb, t // tm), a_args, w_args, side_args,
                     jax.ShapeDtypeStruct((t, d), BF16), pl.BlockSpec((tm, tn), lambda j, i: (i, j)),
                     _epi_merge, [(w.shape[1], tn) for w in ws], 52, "merge")


def _out_proj(merged, w_o, l, x, modrows, ga_off):
    t, d = x.shape
    tm = _pick(t, (1024, 512, 256))
    tn = _pick(d, (512, 256, 128))
    rb = tm // MOD_BLK
    gb = ga_off // tn
    return _fused_mm(
        (d // tn, t // tm),
        [(merged, pl.BlockSpec((tm, d), lambda j, i: (i, 0)))],
        [(w_o, pl.BlockSpec((None, d, tn), lambda j, i: (l, 0, j)))],
        [(x, pl.BlockSpec((tm, tn), lambda j, i: (i, j))),
         (modrows, pl.BlockSpec((None, 1, tn), lambda j, i: (i * rb, 0, gb + j)))],
        jax.ShapeDtypeStruct((t, d), F32), pl.BlockSpec((tm, tn), lambda j, i: (i, j)),
        _epi_residual, [(d, tn)], 48, "out_proj")


def _ffn_up(xs, w_gate, w_up, l):
    e, mx, d = xs.shape
    f = w_gate.shape[-1]
    tm = _pick(mx, (768, 512, 256, 128))
    tf = _pick(f, (256, 128))
    wspec = pl.BlockSpec((None, None, d, tf), lambda ee, j, i: (l, ee, 0, j))
    return _fused_mm(
        (e, f // tf, mx // tm),
        [(xs, pl.BlockSpec((None, tm, d), lambda ee, j, i: (ee, i, 0)))],
        [(w_gate, wspec), (w_up, wspec)], [],
        jax.ShapeDtypeStruct((e, mx, f), BF16), pl.BlockSpec((None, tm, tf), lambda ee, j, i: (ee, i, j)),
        _epi_swiglu, [(d, tf), (d, tf)], 52, "ffn_up")


def _ffn_down(hid, w_down, gate_rows, l):
    e, mx, f = hid.shape
    d = w_down.shape[-1]
    tm = _pick(mx, (768, 512, 256, 128))
    tn = _pick(d, (512, 256, 128))
    return _fused_mm(
        (e, d // tn, mx // tm),
        [(hid, pl.BlockSpec((None, tm, f), lambda ee, j, i: (ee, i, 0)))],
        [(w_down, pl.BlockSpec((None, None, f, tn), lambda ee, j, i: (l, ee, 0, j)))],
        [(gate_rows, pl.BlockSpec((None, tm, 1), lambda ee, j, i: (ee, i, 0)))],
        jax.ShapeDtypeStruct((e, mx, d), BF16), pl.BlockSpec((None, tm, tn), lambda ee, j, i: (ee, i, j)),
        _epi_rowscale, [(f, tn)], 40, "ffn_down")


def _conv_kernel(cb_ref, cc_ref, ch_ref, w_ref, *rest):
    o_ref = rest[-1]
    u = cc_ref[...] * ch_ref[...]
    n = u.shape[0]
    row = lax.broadcasted_iota(jnp.int32, u.shape, 0)
    prev = jnp.where(row == 0, 0.0, pltpu.roll(u, 1, 0))
    nxt = jnp.where(row == n - 1, 0.0, pltpu.roll(u, n - 1, 0))
    z = w_ref[0:1, :] * prev + w_ref[1:2, :] * u + w_ref[2:3, :] * nxt
    o_ref[...] = (cb_ref[...] * z).astype(o_ref.dtype)


def _conv(p, conv_w, l, t, cw, n, nseq, row0, prev=None):
    tc = _pick(cw, (512, 256, 128))
    nb = cw // tc
    rb0 = row0 // n
    in_specs = [pl.BlockSpec((n, tc), lambda b, j: (rb0 + b, j)),
                pl.BlockSpec((n, tc), lambda b, j: (rb0 + b, nb + j)),
                pl.BlockSpec((n, tc), lambda b, j: (rb0 + b, 2 * nb + j)),
                pl.BlockSpec((None, 3, tc), lambda b, j: (l, 0, j))]
    args = [p, p, p, conv_w]
    aliases = {}
    if prev is not None:
        in_specs.append(_any_spec())
        args.append(prev)
        aliases = {4: 0}
    return pl.pallas_call(
        _conv_kernel, out_shape=jax.ShapeDtypeStruct((t, cw), BF16), grid=(nseq, nb),
        in_specs=in_specs, out_specs=pl.BlockSpec((n, tc), lambda b, j: (rb0 + b, j)),
        input_output_aliases=aliases, compiler_params=_params(("arbitrary", "arbitrary"), 40),
        name="conv",
    )(*args)


def _head_norm(x, gain):
    return x * lax.rsqrt(jnp.mean(x * x, axis=-1, keepdims=True) + EPS) * gain


def _rope(x, cos, sin_signed):
    hd = x.shape[-1]
    lane = lax.broadcasted_iota(jnp.int32, x.shape, 1)
    q = hd // 4
    swapped = jnp.where((lane & q) == 0, pltpu.roll(x, hd - q, 1), pltpu.roll(x, q, 1))
    return x * cos + swapped * sin_signed


def _attn_kernel(*refs, groups, hd, latent):
    if latent:
        (q_ref, k_ref, v_ref, qg_ref, kg_ref, cosq_ref, sinq_ref, cosk_ref, sink_ref,
         ck_ref, cv_ref, _, y_ref) = refs
    else:
        q_ref, k_ref, v_ref, qg_ref, kg_ref, y_ref, kout_ref = refs
    scale = hd ** -0.5
    kn = _head_norm(k_ref[...], kg_ref[...])
    if latent:
        kn = _rope(kn, cosk_ref[...], sink_ref[...])
        ckb = ck_ref[...].astype(BF16)
        cvb = cv_ref[...].astype(BF16)
    else:
        kout_ref[...] = kn
    kb = kn.astype(BF16)
    vb = v_ref[...].astype(BF16)
    for g in range(groups):
        qn = _head_norm(q_ref[:, g * hd:(g + 1) * hd], qg_ref[...])
        if latent:
            qn = _rope(qn, cosq_ref[...], sinq_ref[...])
        qb = qn.astype(BF16)
        s = lax.dot_general(qb, kb, NT_DIMS, preferred_element_type=F32) * scale
        m = jnp.max(s, axis=-1, keepdims=True)
        if latent:
            s2 = lax.dot_general(qb, ckb, NT_DIMS, preferred_element_type=F32) * scale
            m = jnp.maximum(m, jnp.max(s2, axis=-1, keepdims=True))
            p2 = jnp.exp(s2 - m)
        p = jnp.exp(s - m)
        den = jnp.sum(p, axis=-1, keepdims=True)
        o = jnp.dot(p.astype(BF16), vb, preferred_element_type=F32)
        if latent:
            den = den + jnp.sum(p2, axis=-1, keepdims=True)
            o = o + jnp.dot(p2.astype(BF16), cvb, preferred_element_type=F32)
        y_ref[:, g * hd:(g + 1) * hd] = (o / den).astype(y_ref.dtype)


def _attn_ctx(p, q_norm, k_norm, l, dims):
    t, n, nseq, nkv, groups, hd = dims["t"], dims["s"], dims["b"], dims["nkv"], dims["groups"], dims["hd"]
    gw = groups * hd
    qb0, kb0, vb0 = dims["o_aq"] // gw, dims["o_ak"] // hd, dims["o_av"] // hd
    return pl.pallas_call(
        functools.partial(_attn_kernel, groups=groups, hd=hd, latent=False),
        out_shape=[jax.ShapeDtypeStruct((t, nkv * gw), BF16), jax.ShapeDtypeStruct((nseq * n, nkv * hd), F32)],
        grid=(nseq, nkv),
        in_specs=[pl.BlockSpec((n, gw), lambda b, kv: (b, qb0 + kv)),
                  pl.BlockSpec((n, hd), lambda b, kv: (b, kb0 + kv)),
                  pl.BlockSpec((n, hd), lambda b, kv: (b, vb0 + kv)),
                  pl.BlockSpec((None, 1, hd), lambda b, kv: (l, 0, 0)),
                  pl.BlockSpec((None, 1, hd), lambda b, kv: (l, 0, 0))],
        out_specs=[pl.BlockSpec((n, gw), lambda b, kv: (b, kv)),
                   pl.BlockSpec((n, hd), lambda b, kv: (b, kv))],
        compiler_params=_params(("arbitrary", "arbitrary"), 40), name="attn_ctx",
    )(p, p, p, q_norm, k_norm)


def _attn_lat(p, q_norm, k_norm, cos, sin, cache_k, cache_v, y_prev, l, dims):
    t, n, nseq, nkv, groups, hd = dims["t"], dims["sd"], dims["bd"], dims["nkv"], dims["groups"], dims["hd"]
    past = cache_k.shape[2]
    gw = groups * hd
    tq = _pick(n, (256, 128))
    nq = n // tq
    qb0, kb0, vb0 = dims["o_aq"] // gw, dims["o_ak"] // hd, dims["o_av"] // hd
    rq0 = dims["row_lat"] // tq
    rk0 = dims["row_lat"] // n
    return pl.pallas_call(
        functools.partial(_attn_kernel, groups=groups, hd=hd, latent=True),
        out_shape=jax.ShapeDtypeStruct((t, nkv * gw), BF16),
        grid=(nseq, nkv, nq),
        in_specs=[pl.BlockSpec((tq, gw), lambda b, kv, i: (rq0 + b * nq + i, qb0 + kv)),
                  pl.BlockSpec((n, hd), lambda b, kv, i: (rk0 + b, kb0 + kv)),
                  pl.BlockSpec((n, hd), lambda b, kv, i: (rk0 + b, vb0 + kv)),
                  pl.BlockSpec((None, 1, hd), lambda b, kv, i: (l, 0, 0)),
                  pl.BlockSpec((None, 1, hd), lambda b, kv, i: (l, 0, 0)),
                  pl.BlockSpec((tq, hd), lambda b, kv, i: (i, 0)),
                  pl.BlockSpec((tq, hd), lambda b, kv, i: (i, 0)),
                  pl.BlockSpec((n, hd), lambda b, kv, i: (0, 0)),
                  pl.BlockSpec((n, hd), lambda b, kv, i: (0, 0)),
                  pl.BlockSpec((None, None, past, hd), lambda b, kv, i: (b, l, 0, kv)),
                  pl.BlockSpec((None, None, past, hd), lambda b, kv, i: (b, l, 0, kv)),
                  _any_spec()],
        out_specs=pl.BlockSpec((tq, gw), lambda b, kv, i: (rq0 + b * nq + i, kv)),
        input_output_aliases={11: 0},
        compiler_params=_params(("arbitrary",) * 3, 40), name="attn_lat",
    )(p, p, p, q_norm, k_norm, cos, sin, cos, sin, cache_k, cache_v, y_prev)


def _log_sigmoid(z):
    return jnp.minimum(z, 0.0) - jnp.log1p(jnp.exp(-jnp.abs(z)))


def _gla_kernel(*refs, n, dk, dv, latent):
    q_ref, k_ref, v_ref, r_ref, lr_ref, w2_ref, b_ref, gn_ref = refs[:8]
    if latent:
        sf0_ref, sb0_ref, _, y_ref, st_ref, o_ref, gf_ref, gb_ref = refs[8:]
    else:
        _, _, y_ref, sf_ref, sb_ref, st_ref, o_ref, gf_ref, gb_ref = refs[8:]
    L = GLA_CHUNK
    nc = n // L
    scale = dk ** -0.5

    lr = lr_ref[...]
    for d, g_ref in ((0, gf_ref), (1, gb_ref)):
        z = jnp.dot(lr, w2_ref[d], preferred_element_type=F32, precision=HIGHEST) + b_ref[d:d + 1, :]
        g_ref[...] = _log_sigmoid(z) / GLA_TAU

    ri = lax.broadcasted_iota(jnp.int32, (L, L), 0)
    ci = lax.broadcasted_iota(jnp.int32, (L, L), 1)

    def run(forward, g_ref, s0_ref, s_out_ref):
        tri = (ri >= ci) if forward else (ri <= ci)
        trif = jnp.where(tri, 1.0, 0.0)
        st_ref[...] = jnp.zeros((dv, dk), F32) if s0_ref is None else s0_ref[...].T

        def body(c, carry):
            r0 = pl.multiple_of((c if forward else nc - 1 - c) * L, L)
            rows = pl.ds(r0, L)
            qc = q_ref[rows, :] * scale
            kc = k_ref[rows, :]
            vb = v_ref[rows, :].astype(BF16)
            b = jnp.dot(trif, g_ref[rows, :], preferred_element_type=F32, precision=HIGHEST)
            bl = b[L - 1:L, :] if forward else b[0:1, :]
            qe = (qc * jnp.exp(b)).astype(BF16)
            ke = (kc * jnp.exp(-b)).astype(BF16)
            a = lax.dot_general(qe, ke, NT_DIMS, preferred_element_type=F32)
            a = jnp.where(tri, a, 0.0).astype(BF16)
            st = st_ref[...]
            o = (lax.dot_general(qe, st.astype(BF16), NT_DIMS, preferred_element_type=F32)
                 + jnp.dot(a, vb, preferred_element_type=F32))
            if forward:
                o_ref[rows, :] = o
            else:
                o_ref[rows, :] += o
            kd = (kc * jnp.exp(bl - b)).astype(BF16)
            st_ref[...] = st * jnp.exp(bl) + lax.dot_general(vb, kd, TN_DIMS, preferred_element_type=F32)
            return carry

        lax.fori_loop(0, nc, body, 0)
        if s_out_ref is not None:
            s_out_ref[...] = st_ref[...].T

    if latent:
        run(True, gf_ref, sf0_ref, None)
        run(False, gb_ref, sb0_ref, None)
    else:
        run(True, gf_ref, None, sf_ref)
        run(False, gb_ref, None, sb_ref)

    o = o_ref[...]
    on = o * lax.rsqrt(jnp.mean(o * o, axis=-1, keepdims=True) + EPS) * gn_ref[...]
    y_ref[...] = (on * _silu(r_ref[...])).astype(y_ref.dtype)


def _gla_common_specs(l, dims, n, rb0):
    dk, dv = dims["dk"], dims["dv"]
    qb0, kb0, vb0, rb_, lb0 = (dims["o_gq"] // dk, dims["o_gk"] // dk, dims["o_gv"] // dv,
                               dims["o_gr"] // dv, dims["o_lr"] // LANES)
    return [pl.BlockSpec((n, dk), lambda b, h: (rb0 + b, qb0 + h)),
            pl.BlockSpec((n, dk), lambda b, h: (rb0 + b, kb0 + h)),
            pl.BlockSpec((n, dv), lambda b, h: (rb0 + b, vb0 + h)),
            pl.BlockSpec((n, dv), lambda b, h: (rb0 + b, rb_ + h)),
            pl.BlockSpec((n, LANES), lambda b, h: (rb0 + b, lb0)),
            pl.BlockSpec((None, 2, LANES, dk), lambda b, h: (l, 0, 0, h)),
            pl.BlockSpec((None, 2, dk), lambda b, h: (l, 0, h)),
            pl.BlockSpec((None, 1, dv), lambda b, h: (l, 0, 0))]


def _gla_scratch(n, dk, dv):
    return [pltpu.VMEM((dv, dk), F32), pltpu.VMEM((n, dv), F32),
            pltpu.VMEM((n, dk), F32), pltpu.VMEM((n, dk), F32)]


def _gla_ctx(p, w2p, gla_b, gla_norm, sf_prev, sb_prev, l, dims):
    t, n, nseq, gh, dk, dv, depth = (dims["t"], dims["s"], dims["b"], dims["gh"], dims["dk"], dims["dv"],
                                     dims["depth"])
    in_specs = _gla_common_specs(l, dims, n, 0)
    args = [p, p, p, p, p, w2p, gla_b, gla_norm]
    st_shape = jax.ShapeDtypeStruct((nseq, depth, gh, dk, dv), F32)
    aliases = {}
    if sf_prev is not None:
        in_specs += [_any_spec(), _any_spec()]
        args += [sf_prev, sb_prev]
        aliases = {8: 1, 9: 2}
    else:
        in_specs += [pl.BlockSpec((None, 1, dv), lambda b, h: (l, 0, 0))] * 2
        args += [gla_norm, gla_norm]
    st_spec = pl.BlockSpec((None, None, None, dk, dv), lambda b, h: (b, l, h, 0, 0))
    return pl.pallas_call(
        functools.partial(_gla_kernel, n=n, dk=dk, dv=dv, latent=False),
        out_shape=[jax.ShapeDtypeStruct((t, gh * dv), BF16), st_shape, st_shape],
        grid=(nseq, gh), in_specs=in_specs,
        out_specs=[pl.BlockSpec((n, dv), lambda b, h: (b, h)), st_spec, st_spec],
        scratch_shapes=_gla_scratch(n, dk, dv), input_output_aliases=aliases,
        compiler_params=_params(("arbitrary", "arbitrary"), 40), name="gla_ctx",
    )(*args)


def _gla_lat(p, w2p, gla_b, gla_norm, state_f, state_b, y_prev, l, dims):
    t, n, nseq, gh, dk, dv = dims["t"], dims["sd"], dims["bd"], dims["gh"], dims["dk"], dims["dv"]
    rb0 = dims["row_lat"] // n
    in_specs = _gla_common_specs(l, dims, n, rb0)
    s_spec = pl.BlockSpec((None, None, None, dk, dv), lambda b, h: (b, l, h, 0, 0))
    in_specs += [s_spec, s_spec, _any_spec()]
    return pl.pallas_call(
        functools.partial(_gla_kernel, n=n, dk=dk, dv=dv, latent=True),
        out_shape=jax.ShapeDtypeStruct((t, gh * dv), BF16),
        grid=(nseq, gh), in_specs=in_specs,
        out_specs=pl.BlockSpec((n, dv), lambda b, h: (rb0 + b, h)),
        scratch_shapes=_gla_scratch(n, dk, dv), input_output_aliases={10: 0},
        compiler_params=_params(("arbitrary", "arbitrary"), 48), name="gla_lat",
    )(p, p, p, p, p, w2p, gla_b, gla_norm, state_f, state_b, y_prev)


def _router_kernel(*refs, n, n_exp, cap, chunk):
    lg_ref, h_ref = refs[:2]
    xs_ref, g_ref, pg_ref, aff_ref, afft_ref = refs[-5:]

    @pl.when(pl.program_id(1) == 0)
    def _route():
        lg = lg_ref[...]
        lane = lax.broadcasted_iota(jnp.int32, lg.shape, 1)
        lg = jnp.where(lane < n_exp, lg, -jnp.inf)
        ex = jnp.exp(lg - jnp.max(lg, axis=-1, keepdims=True))
        aff = ex / jnp.sum(ex, axis=-1, keepdims=True)
        aff_ref[...] = aff
        afft_ref[...] = aff.T
        si = lax.broadcasted_iota(jnp.int32, (chunk, n), 0)
        ti = lax.broadcasted_iota(jnp.int32, (chunk, n), 1)
        slot = lax.broadcasted_iota(jnp.int32, (cap, n), 0).astype(F32)
        for e in range(n_exp):
            row = afft_ref[e:e + 1, :]

            def count(c, acc, e=e, row=row):
                s0 = pl.multiple_of(c * chunk, chunk)
                col = aff_ref[pl.ds(s0, chunk), e:e + 1]
                beats = (col > row) | ((col == row) & (si + s0 < ti))
                return acc + jnp.sum(jnp.where(beats, 1.0, 0.0), axis=0, keepdims=True)

            rank = lax.fori_loop(0, n // chunk, count, jnp.zeros((1, n), F32))
            sel = rank == slot
            pg_ref[e * cap:(e + 1) * cap, :] = jnp.where(sel, 1.0, 0.0).astype(BF16)
            g_ref[e] = jnp.sum(jnp.where(sel, row, 0.0), axis=1, keepdims=True)

    xs = jnp.dot(pg_ref[...], h_ref[...], preferred_element_type=F32)
    for e in range(n_exp):
        xs_ref[e] = xs[e * cap:(e + 1) * cap, :].astype(xs_ref.dtype)


def _router(logits, h2, n, nseq, row0, slot0, mx, n_exp, prev=None):
    t, d = h2.shape
    cap = EC_FACTOR * n // n_exp
    td = _pick(d, (1024, 512, 256, 128))
    rb0 = row0 // n
    sb0 = slot0 // cap
    in_specs = [pl.BlockSpec((n, LANES), lambda b, j: (rb0 + b, 0)),
                pl.BlockSpec((n, td), lambda b, j: (rb0 + b, j))]
    args = [logits, h2]
    aliases = {}
    if prev is not None:
        in_specs += [_any_spec(), _any_spec()]
        args += list(prev)
        aliases = {2: 0, 3: 1}
    return pl.pallas_call(
        functools.partial(_router_kernel, n=n, n_exp=n_exp, cap=cap, chunk=min(n, 128)),
        out_shape=[jax.ShapeDtypeStruct((n_exp, mx, d), BF16), jax.ShapeDtypeStruct((n_exp, mx, 1), F32),
                   jax.ShapeDtypeStruct((nseq, n_exp * cap, n), BF16)],
        grid=(nseq, d // td), in_specs=in_specs,
        out_specs=[pl.BlockSpec((n_exp, cap, td), lambda b, j: (0, sb0 + b, j)),
                   pl.BlockSpec((n_exp, cap, 1), lambda b, j: (0, sb0 + b, 0)),
                   pl.BlockSpec((None, n_exp * cap, n), lambda b, j: (b, 0, 0))],
        scratch_shapes=[pltpu.VMEM((n, LANES), F32), pltpu.VMEM((LANES, n), F32)],
        input_output_aliases=aliases,
        compiler_params=_params(("arbitrary", "arbitrary"), 48), name="router",
    )(*args)


def _scatter_kernel(pg_ref, ye_ref, x_ref, ga_ref, *rest, n_exp, cap):
    o_ref = rest[-1]
    ye = jnp.concatenate([ye_ref[e] for e in range(n_exp)], axis=0)
    y = lax.dot_general(pg_ref[...], ye, TN_DIMS, preferred_element_type=F32)
    o_ref[...] = x_ref[...] + ga_ref[...] * y


def _scatter(pg, ye, x1, modrows, ga_off, n, nseq, row0, slot0, prev=None):
    t, d = x1.shape
    n_exp = ye.shape[0]
    cap = pg.shape[1] // n_exp
    td = _pick(d, (1024, 512, 256, 128))
    rb0 = row0 // n
    sb0 = slot0 // cap
    mb0 = row0 // MOD_BLK
    mstep = n // MOD_BLK
    gb = ga_off // td
    in_specs = [pl.BlockSpec((None, n_exp * cap, n), lambda b, j: (b, 0, 0)),
                pl.BlockSpec((n_exp, cap, td), lambda b, j: (0, sb0 + b, j)),
                pl.BlockSpec((n, td), lambda b, j: (rb0 + b, j)),
                pl.BlockSpec((None, 1, td), lambda b, j: (mb0 + b * mstep, 0, gb + j))]
    args = [pg, ye, x1, modrows]
    aliases = {}
    if prev is not None:
        in_specs.append(_any_spec())
        args.append(prev)
        aliases = {4: 0}
    return pl.pallas_call(
        functools.partial(_scatter_kernel, n_exp=n_exp, cap=cap),
        out_shape=jax.ShapeDtypeStruct((t, d), F32), grid=(nseq, d // td), in_specs=in_specs,
        out_specs=pl.BlockSpec((n, td), lambda b, j: (rb0 + b, j)),
        input_output_aliases=aliases,
        compiler_params=_params(("arbitrary", "arbitrary"), 48), name="scatter",
    )(*args)


def _rope_tables(n, hd):
    qtr = hd // 4
    rows = n // GRID_W
    rpos = jnp.repeat(jnp.arange(rows, dtype=jnp.int32), GRID_W)
    cpos = jnp.arange(rows * GRID_W, dtype=jnp.int32) % GRID_W
    inv = ROPE_THETA ** (-jnp.arange(qtr, dtype=F32) / qtr)
    ar = rpos.astype(F32)[:, None] * inv[None, :]
    ac = cpos.astype(F32)[:, None] * inv[None, :]
    cos = jnp.concatenate([jnp.cos(ar), jnp.cos(ar), jnp.cos(ac), jnp.cos(ac)], axis=-1)
    sin = jnp.concatenate([-jnp.sin(ar), jnp.sin(ar), -jnp.sin(ac), jnp.sin(ac)], axis=-1)
    return cos, sin


def kernel(x_prompt, x_sample, cache_k, cache_v, state_gla_fwd, state_gla_bwd, c, c_ctx, w_mod, b_mod, norm1, norm2, w_in, conv_w, q_norm, k_norm, gla_w2, gla_b, gla_norm, w_conv_out, w_att_out, w_gla_out, w_o, w_router, w_gate, w_up, w_down):
    b, s, d = x_prompt.shape
    bd, sd, _ = x_sample.shape
    depth = w_mod.shape[0]
    cw = conv_w.shape[-1]
    hd = q_norm.shape[-1]
    nkv = cache_k.shape[3]
    aw = w_att_out.shape[1]
    groups = aw // hd // nkv
    gh, dk, dv = state_gla_fwd.shape[2:]
    rank = gla_w2.shape[2]
    n_exp = w_router.shape[-1]
    t = b * s + bd * sd
    row_lat = b * s

    o_aq = 3 * cw
    o_ak = o_aq + aw
    o_av = o_ak + nkv * hd
    o_gq = o_av + nkv * hd
    o_gk = o_gq + gh * dk
    o_gv = o_gk + gh * dk
    o_gr = o_gv + gh * dv
    o_lr = o_gr + gh * dv
    o_m = o_lr + 2 * rank
    tn_p = 512
    n_p = -(-(o_lr + LANES) // tn_p) * tn_p
    assert s % MOD_BLK == 0 and sd % MOD_BLK == 0 and sd % GRID_W == 0
    assert row_lat % sd == 0 and 2 * rank <= LANES and n_p <= w_in.shape[-1] and o_lr % LANES == 0
    dims = dict(t=t, s=s, sd=sd, b=b, bd=bd, nkv=nkv, groups=groups, hd=hd, gh=gh, dk=dk, dv=dv, depth=depth,
                row_lat=row_lat, o_aq=o_aq, o_ak=o_ak, o_av=o_av, o_gq=o_gq, o_gk=o_gk, o_gv=o_gv, o_gr=o_gr,
                o_lr=o_lr)

    cap_c = EC_FACTOR * s // n_exp
    cap_l = EC_FACTOR * sd // n_exp
    mx = b * cap_c + bd * cap_l

    cond8 = jnp.zeros((8, d), F32).at[0].set(c_ctx).at[1:1 + bd].set(c)
    mod_all = _modulation(cond8, w_mod, b_mod)
    blk_row = np.concatenate([np.zeros(b * s // MOD_BLK, np.int32),
                              np.repeat(np.arange(1, bd + 1, dtype=np.int32), sd // MOD_BLK)])

    x = jnp.concatenate([x_prompt.reshape(b * s, d), x_sample.reshape(bd * sd, d)], axis=0)
    cos, sin = _rope_tables(sd, hd)
    w2p = jnp.zeros((depth, 2, LANES, gh * dk), F32)
    w2p = w2p.at[:, 0, :rank].set(gla_w2[:, 0]).at[:, 1, rank:2 * rank].set(gla_w2[:, 1])
    wr_pad = jnp.zeros((depth, d, LANES), F32).at[:, :, :n_exp].set(w_router)
    q_norm3 = q_norm.reshape(depth, 1, hd)
    k_norm3 = k_norm.reshape(depth, 1, hd)
    gla_norm3 = gla_norm.reshape(depth, 1, dv)
    cache_k4 = cache_k.reshape(bd, depth, cache_k.shape[2], nkv * hd)
    cache_v4 = cache_v.reshape(bd, depth, cache_v.shape[2], nkv * hd)
    w_gates = lax.slice_in_dim(w_in, o_m, o_m + 3 * d, axis=2)
    assert (b * cap_c) % cap_l == 0

    new_k, new_v = [], []
    sf = sb = None
    for l in range(depth):
        modrows = mod_all[l][blk_row][:, None, :]
        h = _normmod(x, modrows, norm1[l], 0, d)
        p = _proj(h, w_in, l, n_p, F32, _epi_identity, "in_proj", tn=tn_p)

        y_conv = _conv(p, conv_w, l, t, cw, s, b, 0)
        y_conv = _conv(p, conv_w, l, t, cw, sd, bd, row_lat, prev=y_conv)

        y_att, k_new = _attn_ctx(p, q_norm3, k_norm3, l, dims)
        y_att = _attn_lat(p, q_norm3, k_norm3, cos, sin, cache_k4, cache_v4, y_att, l, dims)
        new_k.append(k_new.reshape(b, s, nkv, hd))
        new_v.append(p[:row_lat, o_av:o_av + nkv * hd].reshape(b, s, nkv, hd))

        y_gla, sf, sb = _gla_ctx(p, w2p, gla_b, gla_norm3, sf, sb, l, dims)
        y_gla = _gla_lat(p, w2p, gla_b, gla_norm3, state_gla_fwd, state_gla_bwd, y_gla, l, dims)

        gates = _proj(h, w_gates, l, 3 * d, F32, _epi_sigmoid, "gate_proj")
        merged = _merge(y_conv, y_att, y_gla, gates, w_conv_out, w_att_out, w_gla_out, l, d)
        x1 = _out_proj(merged, w_o, l, x, modrows, 2 * d)

        h2, logits = _normmod(x1, modrows, norm2[l], 3 * d, 4 * d, w_router_pad=wr_pad[l])
        xs, g_rows, pg_c = _router(logits, h2, s, b, 0, 0, mx, n_exp)
        xs, g_rows, pg_l = _router(logits, h2, sd, bd, row_lat, b * cap_c, mx, n_exp, prev=(xs, g_rows))
        hid = _ffn_up(xs, w_gate, w_up, l)
        ye = _ffn_down(hid, w_down, g_rows, l)
        x2 = _scatter(pg_c, ye, x1, modrows, 5 * d, s, b, 0, 0)
        x = _scatter(pg_l, ye, x1, modrows, 5 * d, sd, bd, row_lat, b * cap_c, prev=x2)

    y_prompt = x[:row_lat].reshape(b, s, d)
    y_sample = x[row_lat:].reshape(bd, sd, d)
    return (y_prompt, y_sample, jnp.stack(new_k, axis=1), jnp.stack(new_v, axis=1), sf, sb)
```

```python
import functools

import jax
import jax.numpy as jnp
import numpy as np
from jax import lax
from jax.experimental import pallas as pl
from jax.experimental.pallas import tpu as pltpu

F32 = jnp.float32
BF16 = jnp.bfloat16
HIGHEST = lax.Precision.HIGHEST

GRID_W = 64
GLA_CHUNK = 64
GLA_TAU = 16.0
EC_FACTOR = 2
ROPE_THETA = 10000.0
EPS = 1e-6

LANES = 128
SUBLANES = 8
MOD_BLK = 256
GLA_GROUP = 256
NT_DIMS = (((1,), (1,)), ((), ()))
TN_DIMS = (((0,), (0,)), ((), ()))


def _params(sem, vmem_mb):
    return pltpu.CompilerParams(dimension_semantics=sem, vmem_limit_bytes=vmem_mb << 20)


def _any_spec():
    return pl.BlockSpec(memory_space=pl.ANY)


def _sigmoid(x):
    return 1.0 / (1.0 + jnp.exp(-x))


def _silu(x):
    return x * _sigmoid(x)


def _pick(total, candidates):
    for c in candidates:
        if total % c == 0:
            return c
    raise ValueError(f"no tile in {candidates} divides {total}")


def _mod_kernel(c_ref, w_ref, b_ref, o_ref):
    s = _silu(c_ref[...]).astype(BF16)
    o_ref[...] = jnp.dot(s, w_ref[...].astype(BF16), preferred_element_type=F32) + b_ref[...]


def _modulation(cond8, w_mod, b_mod):
    depth, d, n6 = w_mod.shape
    tn = _pick(n6, (512, 256, 128))
    return pl.pallas_call(
        _mod_kernel,
        out_shape=jax.ShapeDtypeStruct((depth, 8, n6), F32),
        grid=(depth, n6 // tn),
        in_specs=[pl.BlockSpec((8, d), lambda l, j: (0, 0)),
                  pl.BlockSpec((None, d, tn), lambda l, j: (l, 0, j)),
                  pl.BlockSpec((None, 1, tn), lambda l, j: (l, 0, j))],
        out_specs=pl.BlockSpec((None, 8, tn), lambda l, j: (l, 0, j)),
        compiler_params=_params(("arbitrary", "arbitrary"), 40),
        name="modulation",
    )(cond8, w_mod, b_mod.reshape(depth, 1, n6))


def _normmod_kernel(x_ref, m_ref, g_ref, *rest, d, sh_off, sc_off, router):
    x = x_ref[...]
    y = x * lax.rsqrt(jnp.mean(x * x, axis=-1, keepdims=True) + EPS) * g_ref[...]
    h = y * (1.0 + m_ref[:, sc_off:sc_off + d]) + m_ref[:, sh_off:sh_off + d]
    if router:
        o_ref, lg_ref = rest[-2:]
        lg_ref[...] = jnp.dot(h, rest[0][...], preferred_element_type=F32, precision=HIGHEST)
    else:
        o_ref = rest[-1]
    o_ref[...] = h.astype(BF16)


def _normmod(x_part, row0, t, modrows, gain, sh_off, sc_off, w_router_pad=None, prev=None):
    rows, d = x_part.shape
    router = w_router_pad is not None
    rb0 = row0 // MOD_BLK
    in_specs = [pl.BlockSpec((MOD_BLK, d), lambda i: (i, 0)),
                pl.BlockSpec((None, 1, 6 * d), lambda i: (rb0 + i, 0, 0)),
                pl.BlockSpec((1, d), lambda i: (0, 0))]
    args = [x_part, modrows, gain.reshape(1, d)]
    out_shape = [jax.ShapeDtypeStruct((t, d), BF16)]
    out_specs = [pl.BlockSpec((MOD_BLK, d), lambda i: (rb0 + i, 0))]
    if router:
        in_specs.append(pl.BlockSpec((d, LANES), lambda i: (0, 0)))
        args.append(w_router_pad)
        out_shape.append(jax.ShapeDtypeStruct((t, LANES), F32))
        out_specs.append(pl.BlockSpec((MOD_BLK, LANES), lambda i: (rb0 + i, 0)))
    aliases = {}
    if prev is not None:
        aliases = {len(args) + q: q for q in range(len(prev))}
        in_specs += [_any_spec()] * len(prev)
        args += list(prev)
    return pl.pallas_call(
        functools.partial(_normmod_kernel, d=d, sh_off=sh_off, sc_off=sc_off, router=router),
        out_shape=out_shape, grid=(rows // MOD_BLK,), in_specs=in_specs, out_specs=out_specs,
        input_output_aliases=aliases, compiler_params=_params(("arbitrary",), 40),
        name="normmod_router" if router else "normmod",
    )(*args)


def _mm_kernel(*refs, n_a, n_w, n_sides, w_rows, epilogue):
    a_refs = refs[:n_a]
    w_refs = refs[n_a:n_a + n_w]
    side_refs = refs[n_a + n_w:n_a + n_w + n_sides]
    o_ref = refs[n_a + n_w + n_sides]
    accs = []
    for q, w_ref in enumerate(w_refs):
        a = a_refs[q if n_a > 1 else 0][...]
        if w_rows:
            accs.append(lax.dot_general(a, w_ref[0].astype(BF16), NT_DIMS, preferred_element_type=F32))
        else:
            accs.append(jnp.dot(a, w_ref[...].astype(BF16), preferred_element_type=F32))
    o_ref[...] = epilogue(accs, [s[...] for s in side_refs]).astype(o_ref.dtype)


def _fused_mm(grid, a_args, w_args, side_args, out_shape, out_spec, epilogue, vmem_mb, name, w_rows=False):
    assert len(a_args) in (1, len(w_args))
    arrays = [a for a, _ in a_args] + [w for w, _ in w_args] + [s for s, _ in side_args]
    specs = [s for _, s in a_args] + [s for _, s in w_args] + [s for _, s in side_args]
    body = functools.partial(_mm_kernel, n_a=len(a_args), n_w=len(w_args), n_sides=len(side_args),
                             w_rows=w_rows, epilogue=epilogue)
    return pl.pallas_call(
        body, out_shape=out_shape, grid=grid, in_specs=specs, out_specs=out_spec,
        compiler_params=_params(("arbitrary",) * len(grid), vmem_mb), name=name,
    )(*arrays)


def _epi_identity(accs, sides):
    return accs[0]


def _epi_sigmoid(accs, sides):
    return _sigmoid(accs[0])


def _epi_residual(accs, sides):
    x, ga = sides
    return x + ga * accs[0]


def _epi_swiglu(accs, sides):
    return _silu(accs[0]) * accs[1]


def _epi_merge(accs, sides):
    return sides[0] * accs[0] + sides[1] * accs[1] + sides[2] * accs[2]


def _epi_rowscale(accs, sides):
    return accs[0] * sides[0]


def _proj(h, w_t, l, row0, n_cols, out_dtype, epilogue, name, tn=512):
    t, k = h.shape
    tm = _pick(t, (1024, 512, 256))
    assert row0 % SUBLANES == 0 and n_cols % tn == 0
    elem = (pl.Element(1), pl.Element(tn), pl.Element(k))
    return _fused_mm(
        (n_cols // tn, t // tm),
        [(h, pl.BlockSpec((tm, k), lambda j, i: (i, 0)))],
        [(w_t, pl.BlockSpec(elem, lambda j, i: (l, pl.multiple_of(row0 + j * tn, SUBLANES), 0)))],
        [], jax.ShapeDtypeStruct((t, n_cols), out_dtype),
        pl.BlockSpec((tm, tn), lambda j, i: (i, j)), epilogue, 52, name, w_rows=True)


def _merge(y_conv, y_att, y_gla, gates, w_conv_out, w_att_out, w_gla_out, l, d):
    t = y_conv.shape[0]
    tm = _pick(t, (512, 256))
    tn = _pick(d, (512, 256, 128))
    nb = d // tn
    ys = (y_conv, y_att, y_gla)
    ws = (w_conv_out, w_att_out, w_gla_out)
    a_args = [(y, pl.BlockSpec((tm, y.shape[1]), lambda j, i: (i, 0))) for y in ys]
    w_args = [(w, pl.BlockSpec((None, w.shape[1], tn), lambda j, i: (l, 0, j))) for w in ws]
    side_args = [(gates, pl.BlockSpec((tm, tn), functools.partial(lambda j, i, q: (i, q * nb + j), q=q)))
                 for q in range(3)]
    return _fused_mm((nb, t // tm), a_args, w_args, side_args,
                     jax.ShapeDtypeStruct((t, d), BF16), pl.BlockSpec((tm, tn), lambda j, i: (i, j)),
                     _epi_merge, 48, "merge")


def _out_proj(merged, w_o, l, x_part, row0, modrows, ga_off):
    rows, d = x_part.shape
    tm = _pick(rows, (1024, 512, 256))
    tn = _pick(d, (512, 256, 128))
    assert row0 % tm == 0
    ib0 = row0 // tm
    mb0 = row0 // MOD_BLK
    rb = tm // MOD_BLK
    gb = ga_off // tn
    return _fused_mm(
        (d // tn, rows // tm),
        [(merged, pl.BlockSpec((tm, d), lambda j, i: (ib0 + i, 0)))],
        [(w_o, pl.BlockSpec((None, d, tn), lambda j, i: (l, 0, j)))],
        [(x_part, pl.BlockSpec((tm, tn), lambda j, i: (i, j))),
         (modrows, pl.BlockSpec((None, 1, tn), lambda j, i: (mb0 + i * rb, 0, gb + j)))],
        jax.ShapeDtypeStruct((rows, d), F32), pl.BlockSpec((tm, tn), lambda j, i: (i, j)),
        _epi_residual, 52, "out_proj")


def _ffn_up(xs, w_gate, w_up, l):
    e, mx, d = xs.shape
    f = w_gate.shape[-1]
    tf = _pick(f, (256, 128))
    wspec = pl.BlockSpec((None, None, d, tf), lambda ee, j: (l, ee, 0, j))
    return _fused_mm(
        (e, f // tf),
        [(xs, pl.BlockSpec((None, mx, d), lambda ee, j: (ee, 0, 0)))],
        [(w_gate, wspec), (w_up, wspec)], [],
        jax.ShapeDtypeStruct((e, mx, f), BF16), pl.BlockSpec((None, mx, tf), lambda ee, j: (ee, 0, j)),
        _epi_swiglu, 58, "ffn_up")


def _ffn_down(hid, w_down, gate_rows, l):
    e, mx, f = hid.shape
    d = w_down.shape[-1]
    tn = _pick(d, (512, 256, 128))
    return _fused_mm(
        (e, d // tn),
        [(hid, pl.BlockSpec((None, mx, f), lambda ee, j: (ee, 0, 0)))],
        [(w_down, pl.BlockSpec((None, None, f, tn), lambda ee, j: (l, ee, 0, j)))],
        [(gate_rows, pl.BlockSpec((None, mx, 1), lambda ee, j: (ee, 0, 0)))],
        jax.ShapeDtypeStruct((e, mx, d), BF16), pl.BlockSpec((None, mx, tn), lambda ee, j: (ee, 0, j)),
        _epi_rowscale, 48, "ffn_down")


def _conv_kernel(cb_ref, cc_ref, ch_ref, w_ref, *rest):
    o_ref = rest[-1]
    u = cc_ref[...] * ch_ref[...]
    n = u.shape[0]
    row = lax.broadcasted_iota(jnp.int32, u.shape, 0)
    prev = jnp.where(row == 0, 0.0, pltpu.roll(u, 1, 0))
    nxt = jnp.where(row == n - 1, 0.0, pltpu.roll(u, n - 1, 0))
    z = w_ref[0:1, :] * prev + w_ref[1:2, :] * u + w_ref[2:3, :] * nxt
    o_ref[...] = (cb_ref[...] * z).astype(o_ref.dtype)


def _conv(p, conv_w, l, t, cw, n, nseq, row0, prev=None):
    tc = _pick(cw, (512, 256, 128))
    nb = cw // tc
    rb0 = row0 // n
    in_specs = [pl.BlockSpec((n, tc), lambda b, j: (rb0 + b, j)),
                pl.BlockSpec((n, tc), lambda b, j: (rb0 + b, nb + j)),
                pl.BlockSpec((n, tc), lambda b, j: (rb0 + b, 2 * nb + j)),
                pl.BlockSpec((None, 3, tc), lambda b, j: (l, 0, j))]
    args = [p, p, p, conv_w]
    aliases = {}
    if prev is not None:
        in_specs.append(_any_spec())
        args.append(prev)
        aliases = {4: 0}
    return pl.pallas_call(
        _conv_kernel, out_shape=jax.ShapeDtypeStruct((t, cw), BF16), grid=(nseq, nb),
        in_specs=in_specs, out_specs=pl.BlockSpec((n, tc), lambda b, j: (rb0 + b, j)),
        input_output_aliases=aliases, compiler_params=_params(("arbitrary", "arbitrary"), 40),
        name="conv",
    )(*args)


def _head_norm(x, gain):
    return x * lax.rsqrt(jnp.mean(x * x, axis=-1, keepdims=True) + EPS) * gain


def _rope(x, cos, sin_signed):
    hd = x.shape[-1]
    lane = lax.broadcasted_iota(jnp.int32, x.shape, 1)
    q = hd // 4
    swapped = jnp.where((lane & q) == 0, pltpu.roll(x, hd - q, 1), pltpu.roll(x, q, 1))
    return x * cos + swapped * sin_signed


def _attn_kernel(*refs, groups, hd, latent):
    if latent:
        (q_ref, k_ref, v_ref, qg_ref, kg_ref, cosq_ref, sinq_ref, cosk_ref, sink_ref,
         ck_ref, cv_ref, _, y_ref) = refs
    else:
        q_ref, k_ref, v_ref, qg_ref, kg_ref, y_ref, kout_ref = refs
    scale = hd ** -0.5
    kn = _head_norm(k_ref[...], kg_ref[...])
    if latent:
        kn = _rope(kn, cosk_ref[...], sink_ref[...])
        ckb = ck_ref[...].astype(BF16)
        cvb = cv_ref[...].astype(BF16)
    else:
        kout_ref[...] = kn
    kb = kn.astype(BF16)
    vb = v_ref[...].astype(BF16)
    for g in range(groups):
        qn = _head_norm(q_ref[:, g * hd:(g + 1) * hd], qg_ref[...])
        if latent:
            qn = _rope(qn, cosq_ref[...], sinq_ref[...])
        qb = qn.astype(BF16)
        s = lax.dot_general(qb, kb, NT_DIMS, preferred_element_type=F32) * scale
        m = jnp.max(s, axis=-1, keepdims=True)
        if latent:
            s2 = lax.dot_general(qb, ckb, NT_DIMS, preferred_element_type=F32) * scale
            m = jnp.maximum(m, jnp.max(s2, axis=-1, keepdims=True))
            p2 = jnp.exp(s2 - m)
        p = jnp.exp(s - m)
        den = jnp.sum(p, axis=-1, keepdims=True)
        o = jnp.dot(p.astype(BF16), vb, preferred_element_type=F32)
        if latent:
            den = den + jnp.sum(p2, axis=-1, keepdims=True)
            o = o + jnp.dot(p2.astype(BF16), cvb, preferred_element_type=F32)
        y_ref[:, g * hd:(g + 1) * hd] = (o / den).astype(y_ref.dtype)


def _attn_ctx(p, q_norm, k_norm, l, dims):
    t, n, nseq, nkv, groups, hd = dims["t"], dims["s"], dims["b"], dims["nkv"], dims["groups"], dims["hd"]
    gw = groups * hd
    qb0, kb0, vb0 = dims["o_aq"] // gw, dims["o_ak"] // hd, dims["o_av"] // hd
    return pl.pallas_call(
        functools.partial(_attn_kernel, groups=groups, hd=hd, latent=False),
        out_shape=[jax.ShapeDtypeStruct((t, nkv * gw), BF16), jax.ShapeDtypeStruct((nseq * n, nkv * hd), F32)],
        grid=(nseq, nkv),
        in_specs=[pl.BlockSpec((n, gw), lambda b, kv: (b, qb0 + kv)),
                  pl.BlockSpec((n, hd), lambda b, kv: (b, kb0 + kv)),
                  pl.BlockSpec((n, hd), lambda b, kv: (b, vb0 + kv)),
                  pl.BlockSpec((None, 1, hd), lambda b, kv: (l, 0, 0)),
                  pl.BlockSpec((None, 1, hd), lambda b, kv: (l, 0, 0))],
        out_specs=[pl.BlockSpec((n, gw), lambda b, kv: (b, kv)),
                   pl.BlockSpec((n, hd), lambda b, kv: (b, kv))],
        compiler_params=_params(("arbitrary", "arbitrary"), 40), name="attn_ctx",
    )(p, p, p, q_norm, k_norm)


def _attn_lat(p, q_norm, k_norm, cos, sin, cache_k, cache_v, y_prev, l, dims):
    t, n, nseq, nkv, groups, hd = dims["t"], dims["sd"], dims["bd"], dims["nkv"], dims["groups"], dims["hd"]
    past = cache_k.shape[2]
    gw = groups * hd
    tq = _pick(n, (256, 128))
    nq = n // tq
    qb0, kb0, vb0 = dims["o_aq"] // gw, dims["o_ak"] // hd, dims["o_av"] // hd
    rq0 = dims["row_lat"] // tq
    rk0 = dims["row_lat"] // n
    return pl.pallas_call(
        functools.partial(_attn_kernel, groups=groups, hd=hd, latent=True),
        out_shape=jax.ShapeDtypeStruct((t, nkv * gw), BF16),
        grid=(nseq, nkv, nq),
        in_specs=[pl.BlockSpec((tq, gw), lambda b, kv, i: (rq0 + b * nq + i, qb0 + kv)),
                  pl.BlockSpec((n, hd), lambda b, kv, i: (rk0 + b, kb0 + kv)),
                  pl.BlockSpec((n, hd), lambda b, kv, i: (rk0 + b, vb0 + kv)),
                  pl.BlockSpec((None, 1, hd), lambda b, kv, i: (l, 0, 0)),
                  pl.BlockSpec((None, 1, hd), lambda b, kv, i: (l, 0, 0)),
                  pl.BlockSpec((tq, hd), lambda b, kv, i: (i, 0)),
                  pl.BlockSpec((tq, hd), lambda b, kv, i: (i, 0)),
                  pl.BlockSpec((n, hd), lambda b, kv, i: (0, 0)),
                  pl.BlockSpec((n, hd), lambda b, kv, i: (0, 0)),
                  pl.BlockSpec((None, None, past, hd), lambda b, kv, i: (b, l, 0, kv)),
                  pl.BlockSpec((None, None, past, hd), lambda b, kv, i: (b, l, 0, kv)),
                  _any_spec()],
        out_specs=pl.BlockSpec((tq, gw), lambda b, kv, i: (rq0 + b * nq + i, kv)),
        input_output_aliases={11: 0},
        compiler_params=_params(("arbitrary",) * 3, 40), name="attn_lat",
    )(p, p, p, q_norm, k_norm, cos, sin, cos, sin, cache_k, cache_v, y_prev)


def _log_sigmoid(z):
    return jnp.minimum(z, 0.0) - jnp.log1p(jnp.exp(-jnp.abs(z)))


def _dot01(m01, x):
    x1 = x.astype(BF16)
    r1 = x - x1.astype(F32)
    x2 = r1.astype(BF16)
    x3 = (r1 - x2.astype(F32)).astype(BF16)
    return (jnp.dot(m01, x1, preferred_element_type=F32) + jnp.dot(m01, x2, preferred_element_type=F32)
            + jnp.dot(m01, x3, preferred_element_type=F32))


def _gla_kernel(*refs, n, dk, dv, latent):
    q_ref, k_ref, v_ref, r_ref, lr_ref, w2_ref, b_ref, gn_ref = refs[:8]
    if latent:
        sf0_ref, sb0_ref, _, y_ref = refs[8:12]
        sf_ref = sb_ref = None
    else:
        _, _, y_ref, sf_ref, sb_ref = refs[8:13]
        sf0_ref = sb0_ref = None
    stf_ref, stb_ref, of_ref, ob_ref, gf_ref, gb_ref = refs[-6:]
    L = GLA_CHUNK
    G = min(GLA_GROUP, n)
    cpg = G // L
    ng = n // G
    scale = dk ** -0.5

    lr = lr_ref[...]
    for d, g_ref in ((0, gf_ref), (1, gb_ref)):
        z = jnp.dot(lr, w2_ref[d], preferred_element_type=F32, precision=HIGHEST) + b_ref[d:d + 1, :]
        g_ref[...] = _log_sigmoid(z) / GLA_TAU

    ri = lax.broadcasted_iota(jnp.int32, (G, G), 0)
    ci = lax.broadcasted_iota(jnp.int32, (G, G), 1)
    same_chunk = (ri // L) == (ci // L)

    def run(forward, g_ref, s0_ref, s_out_ref, st_ref, o_ref):
        tri = same_chunk & ((ri >= ci) if forward else (ri <= ci))
        tri01 = jnp.where(tri, 1.0, 0.0).astype(BF16)
        st_ref[...] = jnp.zeros((dv, dk), F32) if s0_ref is None else s0_ref[...].T
        for gi in (range(ng) if forward else reversed(range(ng))):
            r0 = gi * G
            b = _dot01(tri01, g_ref[r0:r0 + G, :])
            b3 = b.reshape(cpg, L, dk)
            bl3 = b3[:, L - 1:L, :] if forward else b3[:, 0:1, :]
            bl = jnp.broadcast_to(bl3, (cpg, L, dk)).reshape(G, dk)
            kc = k_ref[r0:r0 + G, :]
            qe = (q_ref[r0:r0 + G, :] * scale * jnp.exp(b)).astype(BF16)
            ke = (kc * jnp.exp(-b)).astype(BF16)
            kd = (kc * jnp.exp(bl - b)).astype(BF16)
            vb = v_ref[r0:r0 + G, :].astype(BF16)
            a = lax.dot_general(qe, ke, NT_DIMS, preferred_element_type=F32)
            a = jnp.where(tri, a, 0.0).astype(BF16)
            o_intra = jnp.dot(a, vb, preferred_element_type=F32)
            for c in (range(cpg) if forward else reversed(range(cpg))):
                cs = slice(c * L, (c + 1) * L)
                st = st_ref[...]
                o_ref[r0 + c * L:r0 + (c + 1) * L, :] = o_intra[cs] + lax.dot_general(
                    qe[cs], st.astype(BF16), NT_DIMS, preferred_element_type=F32)
                st_ref[...] = st * jnp.exp(bl3[c]) + lax.dot_general(
                    vb[cs], kd[cs], TN_DIMS, preferred_element_type=F32)
        if s_out_ref is not None:
            s_out_ref[...] = st_ref[...].T

    run(True, gf_ref, sf0_ref, sf_ref, stf_ref, of_ref)
    run(False, gb_ref, sb0_ref, sb_ref, stb_ref, ob_ref)

    o = of_ref[...] + ob_ref[...]
    on = o * lax.rsqrt(jnp.mean(o * o, axis=-1, keepdims=True) + EPS) * gn_ref[...]
    y_ref[...] = (on * _silu(r_ref[...])).astype(y_ref.dtype)


def _gla_common_specs(l, dims, n, rb0):
    dk, dv = dims["dk"], dims["dv"]
    qb0, kb0, vb0, rb_, lb0 = (dims["o_gq"] // dk, dims["o_gk"] // dk, dims["o_gv"] // dv,
                               dims["o_gr"] // dv, dims["o_lr"] // LANES)
    return [pl.BlockSpec((n, dk), lambda b, h: (rb0 + b, qb0 + h)),
            pl.BlockSpec((n, dk), lambda b, h: (rb0 + b, kb0 + h)),
            pl.BlockSpec((n, dv), lambda b, h: (rb0 + b, vb0 + h)),
            pl.BlockSpec((n, dv), lambda b, h: (rb0 + b, rb_ + h)),
            pl.BlockSpec((n, LANES), lambda b, h: (rb0 + b, lb0)),
            pl.BlockSpec((None, 2, LANES, dk), lambda b, h: (l, 0, 0, h)),
            pl.BlockSpec((None, 2, dk), lambda b, h: (l, 0, h)),
            pl.BlockSpec((None, 1, dv), lambda b, h: (l, 0, 0))]


def _gla_scratch(n, dk, dv):
    return [pltpu.VMEM((dv, dk), F32), pltpu.VMEM((dv, dk), F32),
            pltpu.VMEM((n, dv), F32), pltpu.VMEM((n, dv), F32),
            pltpu.VMEM((n, dk), F32), pltpu.VMEM((n, dk), F32)]


def _gla_ctx(p, w2p, gla_b, gla_norm, sf_prev, sb_prev, l, dims):
    t, n, nseq, gh, dk, dv, depth = (dims["t"], dims["s"], dims["b"], dims["gh"], dims["dk"], dims["dv"],
                                     dims["depth"])
    in_specs = _gla_common_specs(l, dims, n, 0)
    args = [p, p, p, p, p, w2p, gla_b, gla_norm]
    st_shape = jax.ShapeDtypeStruct((nseq, depth, gh, dk, dv), F32)
    aliases = {}
    if sf_prev is not None:
        in_specs += [_any_spec(), _any_spec()]
        args += [sf_prev, sb_prev]
        aliases = {8: 1, 9: 2}
    else:
        in_specs += [pl.BlockSpec((None, 1, dv), lambda b, h: (l, 0, 0))] * 2
        args += [gla_norm, gla_norm]
    st_spec = pl.BlockSpec((None, None, None, dk, dv), lambda b, h: (b, l, h, 0, 0))
    return pl.pallas_call(
        functools.partial(_gla_kernel, n=n, dk=dk, dv=dv, latent=False),
        out_shape=[jax.ShapeDtypeStruct((t, gh * dv), BF16), st_shape, st_shape],
        grid=(nseq, gh), in_specs=in_specs,
        out_specs=[pl.BlockSpec((n, dv), lambda b, h: (b, h)), st_spec, st_spec],
        scratch_shapes=_gla_scratch(n, dk, dv), input_output_aliases=aliases,
        compiler_params=_params(("arbitrary", "arbitrary"), 40), name="gla_ctx",
    )(*args)


def _gla_lat(p, w2p, gla_b, gla_norm, state_f, state_b, y_prev, l, dims):
    t, n, nseq, gh, dk, dv = dims["t"], dims["sd"], dims["bd"], dims["gh"], dims["dk"], dims["dv"]
    rb0 = dims["row_lat"] // n
    in_specs = _gla_common_specs(l, dims, n, rb0)
    s_spec = pl.BlockSpec((None, None, None, dk, dv), lambda b, h: (b, l, h, 0, 0))
    in_specs += [s_spec, s_spec, _any_spec()]
    return pl.pallas_call(
        functools.partial(_gla_kernel, n=n, dk=dk, dv=dv, latent=True),
        out_shape=jax.ShapeDtypeStruct((t, gh * dv), BF16),
        grid=(nseq, gh), in_specs=in_specs,
        out_specs=pl.BlockSpec((n, dv), lambda b, h: (rb0 + b, h)),
        scratch_shapes=_gla_scratch(n, dk, dv), input_output_aliases={10: 0},
        compiler_params=_params(("arbitrary", "arbitrary"), 48), name="gla_lat",
    )(p, p, p, p, p, w2p, gla_b, gla_norm, state_f, state_b, y_prev)


def _router_kernel(*refs, n, n_exp, cap):
    lg_ref, h_ref = refs[:2]
    xs_ref, g_ref, pg_ref, colb_ref, afft_ref = refs[-5:]
    nt = n // LANES

    @pl.when(pl.program_id(1) == 0)
    def _route():
        lg = lg_ref[...]
        lane = lax.broadcasted_iota(jnp.int32, lg.shape, 1)
        lg = jnp.where(lane < n_exp, lg, -jnp.inf)
        ex = jnp.exp(lg - jnp.max(lg, axis=-1, keepdims=True))
        aff = ex / jnp.sum(ex, axis=-1, keepdims=True)
        afft = aff.T
        for e in range(n_exp):
            afft_ref[e] = afft[e:e + 1, :]
            colb_ref[e] = jnp.broadcast_to(aff[:, e:e + 1], (n, LANES))
        si = lax.broadcasted_iota(jnp.int32, (LANES, LANES), 0)
        ti = lax.broadcasted_iota(jnp.int32, (LANES, LANES), 1)
        earlier = si < ti
        slot = lax.broadcasted_iota(jnp.int32, (cap, n), 0).astype(F32)

        def per_expert(e, carry):
            parts = []
            for tt in range(nt):
                row = afft_ref[e, :, tt * LANES:(tt + 1) * LANES]
                acc = jnp.zeros((LANES, LANES), F32)
                for c in range(nt):
                    col = colb_ref[e, c * LANES:(c + 1) * LANES, :]
                    if c < tt:
                        ahead = jnp.where(col >= row, 1.0, 0.0)
                    elif c > tt:
                        ahead = jnp.where(col > row, 1.0, 0.0)
                    else:
                        ahead = jnp.where(earlier, jnp.where(col >= row, 1.0, 0.0), jnp.where(col > row, 1.0, 0.0))
                    acc = acc + ahead
                parts.append(jnp.sum(acc, axis=0, keepdims=True))
            rank = jnp.concatenate(parts, axis=1)
            sel = rank == slot
            pg_ref[pl.ds(pl.multiple_of(e * cap, cap), cap), :] = jnp.where(sel, 1.0, 0.0).astype(BF16)
            g_ref[e] = jnp.sum(jnp.where(sel, afft_ref[e], 0.0), axis=1, keepdims=True)
            return carry

        lax.fori_loop(0, n_exp, per_expert, 0)

    xs = jnp.dot(pg_ref[...], h_ref[...], preferred_element_type=F32)
    for e in range(n_exp):
        xs_ref[e] = xs[e * cap:(e + 1) * cap, :].astype(xs_ref.dtype)


def _router(logits, h2, n, nseq, row0, slot0, mx, n_exp, prev=None):
    t, d = h2.shape
    cap = EC_FACTOR * n // n_exp
    td = _pick(d, (1024, 512, 256, 128))
    rb0 = row0 // n
    sb0 = slot0 // cap
    in_specs = [pl.BlockSpec((n, LANES), lambda b, j: (rb0 + b, 0)),
                pl.BlockSpec((n, td), lambda b, j: (rb0 + b, j))]
    args = [logits, h2]
    aliases = {}
    if prev is not None:
        in_specs += [_any_spec(), _any_spec()]
        args += list(prev)
        aliases = {2: 0, 3: 1}
    return pl.pallas_call(
        functools.partial(_router_kernel, n=n, n_exp=n_exp, cap=cap),
        out_shape=[jax.ShapeDtypeStruct((n_exp, mx, d), BF16), jax.ShapeDtypeStruct((n_exp, mx, 1), F32),
                   jax.ShapeDtypeStruct((nseq, n_exp * cap, n), BF16)],
        grid=(nseq, d // td), in_specs=in_specs,
        out_specs=[pl.BlockSpec((n_exp, cap, td), lambda b, j: (0, sb0 + b, j)),
                   pl.BlockSpec((n_exp, cap, 1), lambda b, j: (0, sb0 + b, 0)),
                   pl.BlockSpec((None, n_exp * cap, n), lambda b, j: (b, 0, 0))],
        scratch_shapes=[pltpu.VMEM((n_exp, n, LANES), F32), pltpu.VMEM((n_exp, 1, n), F32)],
        input_output_aliases=aliases,
        compiler_params=_params(("arbitrary", "arbitrary"), 48), name="router",
    )(*args)


def _scatter_kernel(pg_ref, ye_ref, x_ref, ga_ref, o_ref, *, n_exp):
    ye = jnp.concatenate([ye_ref[e] for e in range(n_exp)], axis=0)
    y = lax.dot_general(pg_ref[...], ye, TN_DIMS, preferred_element_type=F32)
    o_ref[...] = x_ref[...] + ga_ref[...] * y


def _scatter(pg, ye, x1_part, modrows, ga_off, n, row0, slot0):
    rows, d = x1_part.shape
    nseq = rows // n
    n_exp = ye.shape[0]
    cap = pg.shape[1] // n_exp
    td = _pick(d, (1024, 512, 256, 128))
    sb0 = slot0 // cap
    mb0 = row0 // MOD_BLK
    mstep = n // MOD_BLK
    gb = ga_off // td
    return pl.pallas_call(
        functools.partial(_scatter_kernel, n_exp=n_exp),
        out_shape=jax.ShapeDtypeStruct((rows, d), F32), grid=(nseq, d // td),
        in_specs=[pl.BlockSpec((None, n_exp * cap, n), lambda b, j: (b, 0, 0)),
                  pl.BlockSpec((n_exp, cap, td), lambda b, j: (0, sb0 + b, j)),
                  pl.BlockSpec((n, td), lambda b, j: (b, j)),
                  pl.BlockSpec((None, 1, td), lambda b, j: (mb0 + b * mstep, 0, gb + j))],
        out_specs=pl.BlockSpec((n, td), lambda b, j: (b, j)),
        compiler_params=_params(("arbitrary", "arbitrary"), 48), name="scatter",
    )(pg, ye, x1_part, modrows)


def _rope_tables(n, hd):
    qtr = hd // 4
    rows = n // GRID_W
    rpos = jnp.repeat(jnp.arange(rows, dtype=jnp.int32), GRID_W)
    cpos = jnp.arange(rows * GRID_W, dtype=jnp.int32) % GRID_W
    inv = ROPE_THETA ** (-jnp.arange(qtr, dtype=F32) / qtr)
    ar = rpos.astype(F32)[:, None] * inv[None, :]
    ac = cpos.astype(F32)[:, None] * inv[None, :]
    cos = jnp.concatenate([jnp.cos(ar), jnp.cos(ar), jnp.cos(ac), jnp.cos(ac)], axis=-1)
    sin = jnp.concatenate([-jnp.sin(ar), jnp.sin(ar), -jnp.sin(ac), jnp.sin(ac)], axis=-1)
    return cos, sin


def kernel(x_prompt, x_sample, cache_k, cache_v, state_gla_fwd, state_gla_bwd, c, c_ctx, w_mod, b_mod, norm1, norm2, w_in, conv_w, q_norm, k_norm, gla_w2, gla_b, gla_norm, w_conv_out, w_att_out, w_gla_out, w_o, w_router, w_gate, w_up, w_down):
    b, s, d = x_prompt.shape
    bd, sd, _ = x_sample.shape
    depth = w_mod.shape[0]
    cw = conv_w.shape[-1]
    hd = q_norm.shape[-1]
    nkv = cache_k.shape[3]
    aw = w_att_out.shape[1]
    groups = aw // hd // nkv
    gh, dk, dv = state_gla_fwd.shape[2:]
    rank = gla_w2.shape[2]
    n_exp = w_router.shape[-1]
    t = b * s + bd * sd
    row_lat = b * s

    o_aq = 3 * cw
    o_ak = o_aq + aw
    o_av = o_ak + nkv * hd
    o_gq = o_av + nkv * hd
    o_gk = o_gq + gh * dk
    o_gv = o_gk + gh * dk
    o_gr = o_gv + gh * dv
    o_lr = o_gr + gh * dv
    o_m = o_lr + 2 * rank
    tn_p = 512
    n_p = -(-(o_lr + LANES) // tn_p) * tn_p
    assert s % MOD_BLK == 0 and sd % MOD_BLK == 0 and sd % GRID_W == 0 and s % LANES == 0
    assert row_lat % sd == 0 and 2 * rank <= LANES and n_p <= w_in.shape[-1] and o_lr % LANES == 0
    dims = dict(t=t, s=s, sd=sd, b=b, bd=bd, nkv=nkv, groups=groups, hd=hd, gh=gh, dk=dk, dv=dv, depth=depth,
                row_lat=row_lat, o_aq=o_aq, o_ak=o_ak, o_av=o_av, o_gq=o_gq, o_gk=o_gk, o_gv=o_gv, o_gr=o_gr,
                o_lr=o_lr)

    cap_c = EC_FACTOR * s // n_exp
    cap_l = EC_FACTOR * sd // n_exp
    mx = b * cap_c + bd * cap_l
    assert (b * cap_c) % cap_l == 0

    cond8 = jnp.zeros((8, d), F32).at[0].set(c_ctx).at[1:1 + bd].set(c)
    mod_all = _modulation(cond8, w_mod, b_mod)
    blk_row = np.concatenate([np.zeros(b * s // MOD_BLK, np.int32),
                              np.repeat(np.arange(1, bd + 1, dtype=np.int32), sd // MOD_BLK)])

    x_ctx = x_prompt.reshape(b * s, d)
    x_lat = x_sample.reshape(bd * sd, d)
    cos, sin = _rope_tables(sd, hd)
    w2p = jnp.zeros((depth, 2, LANES, gh * dk), F32)
    w2p = w2p.at[:, 0, :rank].set(gla_w2[:, 0]).at[:, 1, rank:2 * rank].set(gla_w2[:, 1])
    wr_pad = jnp.zeros((depth, d, LANES), F32).at[:, :, :n_exp].set(w_router)
    q_norm3 = q_norm.reshape(depth, 1, hd)
    k_norm3 = k_norm.reshape(depth, 1, hd)
    gla_norm3 = gla_norm.reshape(depth, 1, dv)
    cache_k4 = cache_k.reshape(bd, depth, cache_k.shape[2], nkv * hd)
    cache_v4 = cache_v.reshape(bd, depth, cache_v.shape[2], nkv * hd)
    w_in_t = jnp.swapaxes(w_in, 1, 2)
    w_conv_out_b, w_att_out_b, w_gla_out_b = (w.astype(BF16) for w in (w_conv_out, w_att_out, w_gla_out))

    new_k, new_v = [], []
    sf = sb = None
    for l in range(depth):
        modrows = mod_all[l][blk_row][:, None, :]
        hs = _normmod(x_ctx, 0, t, modrows, norm1[l], 0, d)
        (h,) = _normmod(x_lat, row_lat, t, modrows, norm1[l], 0, d, prev=hs)
        p = _proj(h, w_in_t, l, 0, n_p, F32, _epi_identity, "in_proj", tn=tn_p)

        y_conv = _conv(p, conv_w, l, t, cw, s, b, 0)
        y_conv = _conv(p, conv_w, l, t, cw, sd, bd, row_lat, prev=y_conv)

        y_att, k_new = _attn_ctx(p, q_norm3, k_norm3, l, dims)
        y_att = _attn_lat(p, q_norm3, k_norm3, cos, sin, cache_k4, cache_v4, y_att, l, dims)
        new_k.append(k_new.reshape(b, s, nkv, hd))
        new_v.append(p[:row_lat, o_av:o_av + nkv * hd].reshape(b, s, nkv, hd))

        y_gla, sf, sb = _gla_ctx(p, w2p, gla_b, gla_norm3, sf, sb, l, dims)
        y_gla = _gla_lat(p, w2p, gla_b, gla_norm3, state_gla_fwd, state_gla_bwd, y_gla, l, dims)

        gates = _proj(h, w_in_t, l, o_m, 3 * d, BF16, _epi_sigmoid, "gate_proj")
        merged = _merge(y_conv, y_att, y_gla, gates, w_conv_out_b, w_att_out_b, w_gla_out_b, l, d)
        x1_ctx = _out_proj(merged, w_o, l, x_ctx, 0, modrows, 2 * d)
        x1_lat = _out_proj(merged, w_o, l, x_lat, row_lat, modrows, 2 * d)

        hl = _normmod(x1_ctx, 0, t, modrows, norm2[l], 3 * d, 4 * d, w_router_pad=wr_pad[l])
        h2, logits = _normmod(x1_lat, row_lat, t, modrows, norm2[l], 3 * d, 4 * d, w_router_pad=wr_pad[l], prev=hl)
        xs, g_rows, pg_c = _router(logits, h2, s, b, 0, 0, mx, n_exp)
        xs, g_rows, pg_l = _router(logits, h2, sd, bd, row_lat, b * cap_c, mx, n_exp, prev=(xs, g_rows))
        hid = _ffn_up(xs, w_gate, w_up, l)
        ye = _ffn_down(hid, w_down, g_rows, l)
        x_ctx = _scatter(pg_c, ye, x1_ctx, modrows, 5 * d, s, 0, 0)
        x_lat = _scatter(pg_l, ye, x1_lat, modrows, 5 * d, sd, row_lat, b * cap_c)

    y_prompt = x_ctx.reshape(b, s, d)
    y_sample = x_lat.reshape(bd, sd, d)
    return (y_prompt, y_sample, jnp.stack(new_k, axis=1), jnp.stack(new_v, axis=1), sf, sb)
```

```python
import functools

import jax
import jax.numpy as jnp
import numpy as np
from jax import lax
from jax.experimental import pallas as pl
from jax.experimental.pallas import tpu as pltpu

F32 = jnp.float32
BF16 = jnp.bfloat16

GRID_W = 64
GLA_CHUNK = 64
GLA_TAU = 16.0
EC_FACTOR = 2
ROPE_THETA = 10000.0
EPS = 1e-6

LANES = 128
SUBLANES = 8
MOD_BLK = 256
GLA_GROUP = 256
NT_DIMS = (((1,), (1,)), ((), ()))
TN_DIMS = (((0,), (0,)), ((), ()))


def _params(sem, vmem_mb):
    return pltpu.CompilerParams(dimension_semantics=sem, vmem_limit_bytes=vmem_mb << 20)


def _any_spec():
    return pl.BlockSpec(memory_space=pl.ANY)


def _sigmoid(x):
    return 1.0 / (1.0 + jnp.exp(-x))


def _silu(x):
    return x * _sigmoid(x)


def _dot_split(a, b):
    a1 = a.astype(BF16)
    a2 = (a - a1.astype(F32)).astype(BF16)
    b1 = b.astype(BF16)
    b2 = (b - b1.astype(F32)).astype(BF16)
    return (jnp.dot(a1, b1, preferred_element_type=F32) + jnp.dot(a1, b2, preferred_element_type=F32)
            + jnp.dot(a2, b1, preferred_element_type=F32))


def _pick(total, candidates):
    for c in candidates:
        if total % c == 0:
            return c
    raise ValueError(f"no tile in {candidates} divides {total}")


def _mod_kernel(c_ref, w_ref, b_ref, o_ref):
    s = _silu(c_ref[...]).astype(BF16)
    o_ref[...] = jnp.dot(s, w_ref[...].astype(BF16), preferred_element_type=F32) + b_ref[...]


def _modulation(cond8, w_mod, b_mod):
    depth, d, n6 = w_mod.shape
    tn = _pick(n6, (512, 256, 128))
    return pl.pallas_call(
        _mod_kernel,
        out_shape=jax.ShapeDtypeStruct((depth, 8, n6), F32),
        grid=(depth, n6 // tn),
        in_specs=[pl.BlockSpec((8, d), lambda l, j: (0, 0)),
                  pl.BlockSpec((None, d, tn), lambda l, j: (l, 0, j)),
                  pl.BlockSpec((None, 1, tn), lambda l, j: (l, 0, j))],
        out_specs=pl.BlockSpec((None, 8, tn), lambda l, j: (l, 0, j)),
        compiler_params=_params(("arbitrary", "arbitrary"), 40),
        name="modulation",
    )(cond8, w_mod, b_mod.reshape(depth, 1, n6))


def _normmod_kernel(x_ref, m_ref, g_ref, *rest, d, sh_off, sc_off, router):
    x = x_ref[...]
    y = x * lax.rsqrt(jnp.mean(x * x, axis=-1, keepdims=True) + EPS) * g_ref[...]
    h = y * (1.0 + m_ref[:, sc_off:sc_off + d]) + m_ref[:, sh_off:sh_off + d]
    if router:
        o_ref, lg_ref = rest[-2:]
        lg_ref[...] = _dot_split(h, rest[0][...])
    else:
        o_ref = rest[-1]
    o_ref[...] = h.astype(BF16)


def _normmod(x_part, row0, t, modrows, gain, sh_off, sc_off, w_router_pad=None, prev=None):
    rows, d = x_part.shape
    router = w_router_pad is not None
    nb = _pick(rows, (2 * MOD_BLK, MOD_BLK))
    assert row0 % nb == 0
    rb0 = row0 // nb
    mstep = nb // MOD_BLK
    in_specs = [pl.BlockSpec((nb, d), lambda i: (i, 0)),
                pl.BlockSpec((None, 1, 6 * d), lambda i: ((rb0 + i) * mstep, 0, 0)),
                pl.BlockSpec((1, d), lambda i: (0, 0))]
    args = [x_part, modrows, gain.reshape(1, d)]
    out_shape = [jax.ShapeDtypeStruct((t, d), BF16)]
    out_specs = [pl.BlockSpec((nb, d), lambda i: (rb0 + i, 0))]
    if router:
        in_specs.append(pl.BlockSpec((d, LANES), lambda i: (0, 0)))
        args.append(w_router_pad)
        out_shape.append(jax.ShapeDtypeStruct((t, LANES), F32))
        out_specs.append(pl.BlockSpec((nb, LANES), lambda i: (rb0 + i, 0)))
    aliases = {}
    if prev is not None:
        aliases = {len(args) + q: q for q in range(len(prev))}
        in_specs += [_any_spec()] * len(prev)
        args += list(prev)
    return pl.pallas_call(
        functools.partial(_normmod_kernel, d=d, sh_off=sh_off, sc_off=sc_off, router=router),
        out_shape=out_shape, grid=(rows // nb,), in_specs=in_specs, out_specs=out_specs,
        input_output_aliases=aliases, compiler_params=_params(("arbitrary",), 56),
        name="normmod_router" if router else "normmod",
    )(*args)


def _mm_kernel(*refs, n_a, n_w, n_sides, w_rows, epilogue):
    a_refs = refs[:n_a]
    w_refs = refs[n_a:n_a + n_w]
    side_refs = refs[n_a + n_w:n_a + n_w + n_sides]
    o_ref = refs[n_a + n_w + n_sides]
    accs = []
    for q, w_ref in enumerate(w_refs):
        a = a_refs[q if n_a > 1 else 0][...]
        if w_rows:
            accs.append(lax.dot_general(a, w_ref[0].astype(BF16), NT_DIMS, preferred_element_type=F32))
        else:
            accs.append(jnp.dot(a, w_ref[...].astype(BF16), preferred_element_type=F32))
    o_ref[...] = epilogue(accs, [s[...] for s in side_refs]).astype(o_ref.dtype)


def _fused_mm(grid, a_args, w_args, side_args, out_shape, out_spec, epilogue, vmem_mb, name, w_rows=False):
    assert len(a_args) in (1, len(w_args))
    arrays = [a for a, _ in a_args] + [w for w, _ in w_args] + [s for s, _ in side_args]
    specs = [s for _, s in a_args] + [s for _, s in w_args] + [s for _, s in side_args]
    body = functools.partial(_mm_kernel, n_a=len(a_args), n_w=len(w_args), n_sides=len(side_args),
                             w_rows=w_rows, epilogue=epilogue)
    return pl.pallas_call(
        body, out_shape=out_shape, grid=grid, in_specs=specs, out_specs=out_spec,
        compiler_params=_params(("arbitrary",) * len(grid), vmem_mb), name=name,
    )(*arrays)


def _epi_identity(accs, sides):
    return accs[0]


def _epi_sigmoid(accs, sides):
    return _sigmoid(accs[0])


def _epi_residual(accs, sides):
    x, ga = sides
    return x + ga * accs[0]


def _epi_swiglu(accs, sides):
    return _silu(accs[0]) * accs[1]


def _epi_merge(accs, sides):
    return sides[0] * accs[0] + sides[1] * accs[1] + sides[2] * accs[2]


def _epi_rowscale(accs, sides):
    return accs[0] * sides[0]


def _proj(h, w_t, l, row0, n_cols, out_dtype, epilogue, name, tn=512):
    t, k = h.shape
    tm = _pick(t, (1536, 1024, 512, 256))
    assert row0 % SUBLANES == 0 and n_cols % tn == 0
    elem = (pl.Element(1), pl.Element(tn), pl.Element(k))
    return _fused_mm(
        (n_cols // tn, t // tm),
        [(h, pl.BlockSpec((tm, k), lambda j, i: (i, 0)))],
        [(w_t, pl.BlockSpec(elem, lambda j, i: (l, pl.multiple_of(row0 + j * tn, SUBLANES), 0)))],
        [], jax.ShapeDtypeStruct((t, n_cols), out_dtype),
        pl.BlockSpec((tm, tn), lambda j, i: (i, j)), epilogue, 58, name, w_rows=True)


def _merge(y_conv, y_att, y_gla, gates, w_conv_out, w_att_out, w_gla_out, l, d):
    t = y_conv.shape[0]
    tm = _pick(t, (512, 256))
    tn = _pick(d, (512, 256, 128))
    nb = d // tn
    ys = (y_conv, y_att, y_gla)
    ws = (w_conv_out, w_att_out, w_gla_out)
    a_args = [(y, pl.BlockSpec((tm, y.shape[1]), lambda j, i: (i, 0))) for y in ys]
    w_args = [(w, pl.BlockSpec((None, w.shape[1], tn), lambda j, i: (l, 0, j))) for w in ws]
    side_args = [(gates, pl.BlockSpec((tm, tn), functools.partial(lambda j, i, q: (i, q * nb + j), q=q)))
                 for q in range(3)]
    return _fused_mm((nb, t // tm), a_args, w_args, side_args,
                     jax.ShapeDtypeStruct((t, d), BF16), pl.BlockSpec((tm, tn), lambda j, i: (i, j)),
                     _epi_merge, 48, "merge")


def _out_proj(merged, w_o, l, x_part, row0, modrows, ga_off):
    rows, d = x_part.shape
    tm = _pick(rows, (1024, 512, 256))
    tn = _pick(d, (512, 256, 128))
    assert row0 % tm == 0
    ib0 = row0 // tm
    mb0 = row0 // MOD_BLK
    rb = tm // MOD_BLK
    gb = ga_off // tn
    return _fused_mm(
        (d // tn, rows // tm),
        [(merged, pl.BlockSpec((tm, d), lambda j, i: (ib0 + i, 0)))],
        [(w_o, pl.BlockSpec((None, d, tn), lambda j, i: (l, 0, j)))],
        [(x_part, pl.BlockSpec((tm, tn), lambda j, i: (i, j))),
         (modrows, pl.BlockSpec((None, 1, tn), lambda j, i: (mb0 + i * rb, 0, gb + j)))],
        jax.ShapeDtypeStruct((rows, d), F32), pl.BlockSpec((tm, tn), lambda j, i: (i, j)),
        _epi_residual, 52, "out_proj")


def _ffn_up(xs, w_gate, w_up, l):
    e, mx, d = xs.shape
    f = w_gate.shape[-1]
    tf = _pick(f, (256, 128))
    wspec = pl.BlockSpec((None, None, d, tf), lambda ee, j: (l, ee, 0, j))
    return _fused_mm(
        (e, f // tf),
        [(xs, pl.BlockSpec((None, mx, d), lambda ee, j: (ee, 0, 0)))],
        [(w_gate, wspec), (w_up, wspec)], [],
        jax.ShapeDtypeStruct((e, mx, f), BF16), pl.BlockSpec((None, mx, tf), lambda ee, j: (ee, 0, j)),
        _epi_swiglu, 58, "ffn_up")


def _ffn_down(hid, w_down, gate_rows, l):
    e, mx, f = hid.shape
    d = w_down.shape[-1]
    tn = _pick(d, (1024, 512, 256, 128))
    return _fused_mm(
        (e, d // tn),
        [(hid, pl.BlockSpec((None, mx, f), lambda ee, j: (ee, 0, 0)))],
        [(w_down, pl.BlockSpec((None, None, f, tn), lambda ee, j: (l, ee, 0, j)))],
        [(gate_rows, pl.BlockSpec((None, mx, 1), lambda ee, j: (ee, 0, 0)))],
        jax.ShapeDtypeStruct((e, mx, d), BF16), pl.BlockSpec((None, mx, tn), lambda ee, j: (ee, 0, j)),
        _epi_rowscale, 56, "ffn_down")


def _conv_kernel(cb_ref, cc_ref, ch_ref, w_ref, o_ref, *, ctx_blocks, s, sd):
    u = cc_ref[...] * ch_ref[...]
    n = u.shape[0]
    row = lax.broadcasted_iota(jnp.int32, u.shape, 0)
    last = jnp.where(pl.program_id(0) < ctx_blocks, s - 1, sd - 1)
    pos = row & last
    prev = jnp.where(pos == 0, 0.0, pltpu.roll(u, 1, 0))
    nxt = jnp.where(pos == last, 0.0, pltpu.roll(u, n - 1, 0))
    z = w_ref[0:1, :] * prev + w_ref[1:2, :] * u + w_ref[2:3, :] * nxt
    o_ref[...] = (cb_ref[...] * z).astype(o_ref.dtype)


def _conv(p, conv_w, l, t, cw, s, sd, row_lat):
    assert s & (s - 1) == 0 and sd & (sd - 1) == 0 and sd % s == 0 and row_lat % sd == 0
    tc = _pick(cw, (512, 256, 128))
    nb = cw // tc
    return pl.pallas_call(
        functools.partial(_conv_kernel, ctx_blocks=row_lat // sd, s=s, sd=sd),
        out_shape=jax.ShapeDtypeStruct((t, cw), BF16), grid=(t // sd, nb),
        in_specs=[pl.BlockSpec((sd, tc), lambda b, j: (b, j)),
                  pl.BlockSpec((sd, tc), lambda b, j: (b, nb + j)),
                  pl.BlockSpec((sd, tc), lambda b, j: (b, 2 * nb + j)),
                  pl.BlockSpec((None, 3, tc), lambda b, j: (l, 0, j))],
        out_specs=pl.BlockSpec((sd, tc), lambda b, j: (b, j)),
        compiler_params=_params(("arbitrary", "arbitrary"), 40), name="conv",
    )(p, p, p, conv_w)


def _head_norm(x, gain):
    return x * lax.rsqrt(jnp.mean(x * x, axis=-1, keepdims=True) + EPS) * gain


def _rope(x, cos, sin_signed):
    hd = x.shape[-1]
    lane = lax.broadcasted_iota(jnp.int32, x.shape, 1)
    q = hd // 4
    swapped = jnp.where((lane & q) == 0, pltpu.roll(x, hd - q, 1), pltpu.roll(x, q, 1))
    return x * cos + swapped * sin_signed


def _attn_kernel(*refs, nkv, groups, hd, latent):
    if latent:
        (q_ref, k_ref, v_ref, qg_ref, kg_ref, cosq_ref, sinq_ref, cosk_ref, sink_ref,
         ck_ref, cv_ref, _, y_ref) = refs
    else:
        q_ref, k_ref, v_ref, qg_ref, kg_ref, y_ref, kout_ref = refs
    scale = hd ** -0.5
    for kv in range(nkv):
        ks = slice(kv * hd, (kv + 1) * hd)
        kn = _head_norm(k_ref[:, ks], kg_ref[...])
        if latent:
            kn = _rope(kn, cosk_ref[...], sink_ref[...])
            ckb = ck_ref[:, ks].astype(BF16)
            cvb = cv_ref[:, ks].astype(BF16)
        else:
            kout_ref[:, ks] = kn
        kb = kn.astype(BF16)
        vb = v_ref[:, ks].astype(BF16)
        for g in range(groups):
            qs = slice((kv * groups + g) * hd, (kv * groups + g + 1) * hd)
            qn = _head_norm(q_ref[:, qs], qg_ref[...])
            if latent:
                qn = _rope(qn, cosq_ref[...], sinq_ref[...])
            qb = qn.astype(BF16)
            s = lax.dot_general(qb, kb, NT_DIMS, preferred_element_type=F32) * scale
            m = jnp.max(s, axis=-1, keepdims=True)
            if latent:
                s2 = lax.dot_general(qb, ckb, NT_DIMS, preferred_element_type=F32) * scale
                m = jnp.maximum(m, jnp.max(s2, axis=-1, keepdims=True))
                p2 = jnp.exp(s2 - m)
            p = jnp.exp(s - m)
            den = jnp.sum(p, axis=-1, keepdims=True)
            o = jnp.dot(p.astype(BF16), vb, preferred_element_type=F32)
            if latent:
                den = den + jnp.sum(p2, axis=-1, keepdims=True)
                o = o + jnp.dot(p2.astype(BF16), cvb, preferred_element_type=F32)
            y_ref[:, qs] = (o / den).astype(y_ref.dtype)


def _attn_ctx(p, q_norm, k_norm, l, dims):
    t, n, nseq, nkv, groups, hd = dims["t"], dims["s"], dims["b"], dims["nkv"], dims["groups"], dims["hd"]
    aw, kw = nkv * groups * hd, nkv * hd
    assert dims["o_aq"] % aw == 0 and dims["o_ak"] % kw == 0 and dims["o_av"] % kw == 0
    qb0, kb0, vb0 = dims["o_aq"] // aw, dims["o_ak"] // kw, dims["o_av"] // kw
    return pl.pallas_call(
        functools.partial(_attn_kernel, nkv=nkv, groups=groups, hd=hd, latent=False),
        out_shape=[jax.ShapeDtypeStruct((t, aw), BF16), jax.ShapeDtypeStruct((nseq * n, kw), F32)],
        grid=(nseq,),
        in_specs=[pl.BlockSpec((n, aw), lambda b: (b, qb0)),
                  pl.BlockSpec((n, kw), lambda b: (b, kb0)),
                  pl.BlockSpec((n, kw), lambda b: (b, vb0)),
                  pl.BlockSpec((None, 1, hd), lambda b: (l, 0, 0)),
                  pl.BlockSpec((None, 1, hd), lambda b: (l, 0, 0))],
        out_specs=[pl.BlockSpec((n, aw), lambda b: (b, 0)),
                   pl.BlockSpec((n, kw), lambda b: (b, 0))],
        compiler_params=_params(("arbitrary",), 40), name="attn_ctx",
    )(p, p, p, q_norm, k_norm)


def _attn_lat(p, q_norm, k_norm, cos, sin, cache_k, cache_v, y_prev, l, dims):
    t, n, nseq, nkv, groups, hd = dims["t"], dims["sd"], dims["bd"], dims["nkv"], dims["groups"], dims["hd"]
    past = cache_k.shape[2]
    aw, kw = nkv * groups * hd, nkv * hd
    tq = _pick(n, (256, 128))
    nq = n // tq
    qb0, kb0, vb0 = dims["o_aq"] // aw, dims["o_ak"] // kw, dims["o_av"] // kw
    rq0 = dims["row_lat"] // tq
    rk0 = dims["row_lat"] // n
    return pl.pallas_call(
        functools.partial(_attn_kernel, nkv=nkv, groups=groups, hd=hd, latent=True),
        out_shape=jax.ShapeDtypeStruct((t, aw), BF16),
        grid=(nseq, nq),
        in_specs=[pl.BlockSpec((tq, aw), lambda b, i: (rq0 + b * nq + i, qb0)),
                  pl.BlockSpec((n, kw), lambda b, i: (rk0 + b, kb0)),
                  pl.BlockSpec((n, kw), lambda b, i: (rk0 + b, vb0)),
                  pl.BlockSpec((None, 1, hd), lambda b, i: (l, 0, 0)),
                  pl.BlockSpec((None, 1, hd), lambda b, i: (l, 0, 0)),
                  pl.BlockSpec((tq, hd), lambda b, i: (i, 0)),
                  pl.BlockSpec((tq, hd), lambda b, i: (i, 0)),
                  pl.BlockSpec((n, hd), lambda b, i: (0, 0)),
                  pl.BlockSpec((n, hd), lambda b, i: (0, 0)),
                  pl.BlockSpec((None, None, past, kw), lambda b, i: (b, l, 0, 0)),
                  pl.BlockSpec((None, None, past, kw), lambda b, i: (b, l, 0, 0)),
                  _any_spec()],
        out_specs=pl.BlockSpec((tq, aw), lambda b, i: (rq0 + b * nq + i, 0)),
        input_output_aliases={11: 0},
        compiler_params=_params(("arbitrary",) * 2, 48), name="attn_lat",
    )(p, p, p, q_norm, k_norm, cos, sin, cos, sin, cache_k, cache_v, y_prev)


def _log_sigmoid(z):
    return jnp.minimum(z, 0.0) - jnp.log1p(jnp.exp(-jnp.abs(z)))


def _dot01(m01, x):
    x1 = x.astype(BF16)
    r1 = x - x1.astype(F32)
    x2 = r1.astype(BF16)
    x3 = (r1 - x2.astype(F32)).astype(BF16)
    return (jnp.dot(m01, x1, preferred_element_type=F32) + jnp.dot(m01, x2, preferred_element_type=F32)
            + jnp.dot(m01, x3, preferred_element_type=F32))


def _gla_kernel(*refs, n, dk, dv, latent):
    q_ref, k_ref, v_ref, r_ref, lr_ref, w2_ref, b_ref, gn_ref = refs[:8]
    if latent:
        sf0_ref, sb0_ref, _, y_ref = refs[8:12]
        sf_ref = sb_ref = None
    else:
        _, _, y_ref, sf_ref, sb_ref = refs[8:13]
        sf0_ref = sb0_ref = None
    stf_ref, stb_ref, of_ref, ob_ref, gf_ref, gb_ref = refs[-6:]
    L = GLA_CHUNK
    G = min(GLA_GROUP, n)
    cpg = G // L
    ng = n // G
    scale = dk ** -0.5

    lr = lr_ref[...]
    for d, g_ref in ((0, gf_ref), (1, gb_ref)):
        z = _dot_split(lr, w2_ref[d]) + b_ref[d:d + 1, :]
        g_ref[...] = _log_sigmoid(z) / GLA_TAU

    ri = lax.broadcasted_iota(jnp.int32, (G, G), 0)
    ci = lax.broadcasted_iota(jnp.int32, (G, G), 1)
    same_chunk = (ri // L) == (ci // L)

    def run(forward, g_ref, s0_ref, s_out_ref, st_ref, o_ref):
        tri = same_chunk & ((ri >= ci) if forward else (ri <= ci))
        tri01 = jnp.where(tri, 1.0, 0.0).astype(BF16)
        st_ref[...] = jnp.zeros((dv, dk), F32) if s0_ref is None else s0_ref[...].T
        for gi in (range(ng) if forward else reversed(range(ng))):
            r0 = gi * G
            b = _dot01(tri01, g_ref[r0:r0 + G, :])
            b3 = b.reshape(cpg, L, dk)
            bl3 = b3[:, L - 1:L, :] if forward else b3[:, 0:1, :]
            bl = jnp.broadcast_to(bl3, (cpg, L, dk)).reshape(G, dk)
            kc = k_ref[r0:r0 + G, :]
            qe = (q_ref[r0:r0 + G, :] * scale * jnp.exp(b)).astype(BF16)
            ke = (kc * jnp.exp(-b)).astype(BF16)
            kd = (kc * jnp.exp(bl - b)).astype(BF16)
            vb = v_ref[r0:r0 + G, :].astype(BF16)
            a = lax.dot_general(qe, ke, NT_DIMS, preferred_element_type=F32)
            a = jnp.where(tri, a, 0.0).astype(BF16)
            o_intra = jnp.dot(a, vb, preferred_element_type=F32)
            for c in (range(cpg) if forward else reversed(range(cpg))):
                cs = slice(c * L, (c + 1) * L)
                st = st_ref[...]
                o_ref[r0 + c * L:r0 + (c + 1) * L, :] = o_intra[cs] + lax.dot_general(
                    qe[cs], st.astype(BF16), NT_DIMS, preferred_element_type=F32)
                st_ref[...] = st * jnp.exp(bl3[c]) + lax.dot_general(
                    vb[cs], kd[cs], TN_DIMS, preferred_element_type=F32)
        if s_out_ref is not None:
            s_out_ref[...] = st_ref[...].T

    run(True, gf_ref, sf0_ref, sf_ref, stf_ref, of_ref)
    run(False, gb_ref, sb0_ref, sb_ref, stb_ref, ob_ref)

    o = of_ref[...] + ob_ref[...]
    on = o * lax.rsqrt(jnp.mean(o * o, axis=-1, keepdims=True) + EPS) * gn_ref[...]
    y_ref[...] = (on * _silu(r_ref[...])).astype(y_ref.dtype)


def _gla_common_specs(l, dims, n, rb0):
    dk, dv = dims["dk"], dims["dv"]
    qb0, kb0, vb0, rb_ = dims["o_gq"] // dk, dims["o_gk"] // dk, dims["o_gv"] // dv, dims["o_gr"] // dv
    return [pl.BlockSpec((n, dk), lambda b, h: (rb0 + b, qb0 + h)),
            pl.BlockSpec((n, dk), lambda b, h: (rb0 + b, kb0 + h)),
            pl.BlockSpec((n, dv), lambda b, h: (rb0 + b, vb0 + h)),
            pl.BlockSpec((n, dv), lambda b, h: (rb0 + b, rb_ + h)),
            pl.BlockSpec((n, LANES), lambda b, h: (rb0 + b, 0)),
            pl.BlockSpec((None, 2, LANES, dk), lambda b, h: (l, 0, 0, h)),
            pl.BlockSpec((None, 2, dk), lambda b, h: (l, 0, h)),
            pl.BlockSpec((None, 1, dv), lambda b, h: (l, 0, 0))]


def _gla_scratch(n, dk, dv):
    return [pltpu.VMEM((dv, dk), F32), pltpu.VMEM((dv, dk), F32),
            pltpu.VMEM((n, dv), F32), pltpu.VMEM((n, dv), F32),
            pltpu.VMEM((n, dk), F32), pltpu.VMEM((n, dk), F32)]


def _gla_ctx(p, plr, w2p, gla_b, gla_norm, sf_prev, sb_prev, l, dims):
    t, n, nseq, gh, dk, dv, depth = (dims["t"], dims["s"], dims["b"], dims["gh"], dims["dk"], dims["dv"],
                                     dims["depth"])
    in_specs = _gla_common_specs(l, dims, n, 0)
    args = [p, p, p, p, plr, w2p, gla_b, gla_norm]
    st_shape = jax.ShapeDtypeStruct((nseq, depth, gh, dk, dv), F32)
    aliases = {}
    if sf_prev is not None:
        in_specs += [_any_spec(), _any_spec()]
        args += [sf_prev, sb_prev]
        aliases = {8: 1, 9: 2}
    else:
        in_specs += [pl.BlockSpec((None, 1, dv), lambda b, h: (l, 0, 0))] * 2
        args += [gla_norm, gla_norm]
    st_spec = pl.BlockSpec((None, None, None, dk, dv), lambda b, h: (b, l, h, 0, 0))
    return pl.pallas_call(
        functools.partial(_gla_kernel, n=n, dk=dk, dv=dv, latent=False),
        out_shape=[jax.ShapeDtypeStruct((t, gh * dv), BF16), st_shape, st_shape],
        grid=(nseq, gh), in_specs=in_specs,
        out_specs=[pl.BlockSpec((n, dv), lambda b, h: (b, h)), st_spec, st_spec],
        scratch_shapes=_gla_scratch(n, dk, dv), input_output_aliases=aliases,
        compiler_params=_params(("arbitrary", "arbitrary"), 40), name="gla_ctx",
    )(*args)


def _gla_lat(p, plr, w2p, gla_b, gla_norm, state_f, state_b, y_prev, l, dims):
    t, n, nseq, gh, dk, dv = dims["t"], dims["sd"], dims["bd"], dims["gh"], dims["dk"], dims["dv"]
    rb0 = dims["row_lat"] // n
    in_specs = _gla_common_specs(l, dims, n, rb0)
    s_spec = pl.BlockSpec((None, None, None, dk, dv), lambda b, h: (b, l, h, 0, 0))
    in_specs += [s_spec, s_spec, _any_spec()]
    return pl.pallas_call(
        functools.partial(_gla_kernel, n=n, dk=dk, dv=dv, latent=True),
        out_shape=jax.ShapeDtypeStruct((t, gh * dv), BF16),
        grid=(nseq, gh), in_specs=in_specs,
        out_specs=pl.BlockSpec((n, dv), lambda b, h: (rb0 + b, h)),
        scratch_shapes=_gla_scratch(n, dk, dv), input_output_aliases={10: 0},
        compiler_params=_params(("arbitrary", "arbitrary"), 48), name="gla_lat",
    )(p, p, p, p, plr, w2p, gla_b, gla_norm, state_f, state_b, y_prev)


SEQ_TILE_ELEMS = 1 << 20


def _seq_col_tile(n, d):
    return _pick(d, tuple(c for c in (4096, 2048, 1024, 512, 256, 128) if n * c <= SEQ_TILE_ELEMS))


def _router_kernel(*refs, n, n_exp, cap):
    lg_ref, h_ref = refs[:2]
    xs_ref, g_ref, pg_ref, colb_ref, afft_ref = refs[-5:]
    nt = n // LANES

    @pl.when(pl.program_id(1) == 0)
    def _route():
        lg = lg_ref[...]
        lane = lax.broadcasted_iota(jnp.int32, lg.shape, 1)
        lg = jnp.where(lane < n_exp, lg, -jnp.inf)
        ex = jnp.exp(lg - jnp.max(lg, axis=-1, keepdims=True))
        aff = ex / jnp.sum(ex, axis=-1, keepdims=True)
        afft = aff.T
        for e in range(n_exp):
            afft_ref[e] = afft[e:e + 1, :]
            colb_ref[e] = jnp.broadcast_to(aff[:, e:e + 1], (n, LANES))
        si = lax.broadcasted_iota(jnp.int32, (LANES, LANES), 0)
        ti = lax.broadcasted_iota(jnp.int32, (LANES, LANES), 1)
        earlier = si < ti
        slot = lax.broadcasted_iota(jnp.int32, (cap, n), 0).astype(F32)

        def per_expert(e, carry):
            parts = []
            for tt in range(nt):
                row = afft_ref[e, :, tt * LANES:(tt + 1) * LANES]
                acc = jnp.zeros((LANES, LANES), F32)
                for c in range(nt):
                    col = colb_ref[e, c * LANES:(c + 1) * LANES, :]
                    if c < tt:
                        ahead = jnp.where(col >= row, 1.0, 0.0)
                    elif c > tt:
                        ahead = jnp.where(col > row, 1.0, 0.0)
                    else:
                        ahead = jnp.where(earlier, jnp.where(col >= row, 1.0, 0.0), jnp.where(col > row, 1.0, 0.0))
                    acc = acc + ahead
                parts.append(jnp.sum(acc, axis=0, keepdims=True))
            rank = jnp.concatenate(parts, axis=1)
            sel = rank == slot
            pg_ref[pl.ds(pl.multiple_of(e * cap, cap), cap), :] = jnp.where(sel, 1.0, 0.0).astype(BF16)
            g_ref[e] = jnp.sum(jnp.where(sel, afft_ref[e], 0.0), axis=1, keepdims=True)
            return carry

        lax.fori_loop(0, n_exp, per_expert, 0)

    xs = jnp.dot(pg_ref[...], h_ref[...], preferred_element_type=F32)
    for e in range(n_exp):
        xs_ref[e] = xs[e * cap:(e + 1) * cap, :].astype(xs_ref.dtype)


def _router(logits, h2, n, nseq, row0, slot0, mx, n_exp, prev=None):
    t, d = h2.shape
    cap = EC_FACTOR * n // n_exp
    td = _seq_col_tile(n, d)
    rb0 = row0 // n
    sb0 = slot0 // cap
    in_specs = [pl.BlockSpec((n, LANES), lambda b, j: (rb0 + b, 0)),
                pl.BlockSpec((n, td), lambda b, j: (rb0 + b, j))]
    args = [logits, h2]
    aliases = {}
    if prev is not None:
        in_specs += [_any_spec(), _any_spec()]
        args += list(prev)
        aliases = {2: 0, 3: 1}
    return pl.pallas_call(
        functools.partial(_router_kernel, n=n, n_exp=n_exp, cap=cap),
        out_shape=[jax.ShapeDtypeStruct((n_exp, mx, d), BF16), jax.ShapeDtypeStruct((n_exp, mx, 1), F32),
                   jax.ShapeDtypeStruct((nseq, n_exp * cap, n), BF16)],
        grid=(nseq, d // td), in_specs=in_specs,
        out_specs=[pl.BlockSpec((n_exp, cap, td), lambda b, j: (0, sb0 + b, j)),
                   pl.BlockSpec((n_exp, cap, 1), lambda b, j: (0, sb0 + b, 0)),
                   pl.BlockSpec((None, n_exp * cap, n), lambda b, j: (b, 0, 0))],
        scratch_shapes=[pltpu.VMEM((n_exp, n, LANES), F32), pltpu.VMEM((n_exp, 1, n), F32)],
        input_output_aliases=aliases,
        compiler_params=_params(("arbitrary", "arbitrary"), 48), name="router",
    )(*args)


def _scatter_kernel(pg_ref, ye_ref, x_ref, ga_ref, o_ref, *, n_exp):
    ye = jnp.concatenate([ye_ref[e] for e in range(n_exp)], axis=0)
    y = lax.dot_general(pg_ref[...], ye, TN_DIMS, preferred_element_type=F32)
    o_ref[...] = x_ref[...] + ga_ref[...] * y


def _scatter(pg, ye, x1_part, modrows, ga_off, n, row0, slot0):
    rows, d = x1_part.shape
    nseq = rows // n
    n_exp = ye.shape[0]
    cap = pg.shape[1] // n_exp
    td = _seq_col_tile(n, d)
    sb0 = slot0 // cap
    mb0 = row0 // MOD_BLK
    mstep = n // MOD_BLK
    gb = ga_off // td
    return pl.pallas_call(
        functools.partial(_scatter_kernel, n_exp=n_exp),
        out_shape=jax.ShapeDtypeStruct((rows, d), F32), grid=(nseq, d // td),
        in_specs=[pl.BlockSpec((None, n_exp * cap, n), lambda b, j: (b, 0, 0)),
                  pl.BlockSpec((n_exp, cap, td), lambda b, j: (0, sb0 + b, j)),
                  pl.BlockSpec((n, td), lambda b, j: (b, j)),
                  pl.BlockSpec((None, 1, td), lambda b, j: (mb0 + b * mstep, 0, gb + j))],
        out_specs=pl.BlockSpec((n, td), lambda b, j: (b, j)),
        compiler_params=_params(("arbitrary", "arbitrary"), 48), name="scatter",
    )(pg, ye, x1_part, modrows)


def _rope_tables(n, hd):
    qtr = hd // 4
    rows = n // GRID_W
    rpos = jnp.repeat(jnp.arange(rows, dtype=jnp.int32), GRID_W)
    cpos = jnp.arange(rows * GRID_W, dtype=jnp.int32) % GRID_W
    inv = ROPE_THETA ** (-jnp.arange(qtr, dtype=F32) / qtr)
    ar = rpos.astype(F32)[:, None] * inv[None, :]
    ac = cpos.astype(F32)[:, None] * inv[None, :]
    cos = jnp.concatenate([jnp.cos(ar), jnp.cos(ar), jnp.cos(ac), jnp.cos(ac)], axis=-1)
    sin = jnp.concatenate([-jnp.sin(ar), jnp.sin(ar), -jnp.sin(ac), jnp.sin(ac)], axis=-1)
    return cos, sin


def kernel(x_prompt, x_sample, cache_k, cache_v, state_gla_fwd, state_gla_bwd, c, c_ctx, w_mod, b_mod, norm1, norm2, w_in, conv_w, q_norm, k_norm, gla_w2, gla_b, gla_norm, w_conv_out, w_att_out, w_gla_out, w_o, w_router, w_gate, w_up, w_down):
    b, s, d = x_prompt.shape
    bd, sd, _ = x_sample.shape
    depth = w_mod.shape[0]
    cw = conv_w.shape[-1]
    hd = q_norm.shape[-1]
    nkv = cache_k.shape[3]
    aw = w_att_out.shape[1]
    groups = aw // hd // nkv
    gh, dk, dv = state_gla_fwd.shape[2:]
    rank = gla_w2.shape[2]
    n_exp = w_router.shape[-1]
    t = b * s + bd * sd
    row_lat = b * s

    o_aq = 3 * cw
    o_ak = o_aq + aw
    o_av = o_ak + nkv * hd
    o_gq = o_av + nkv * hd
    o_gk = o_gq + gh * dk
    o_gv = o_gk + gh * dk
    o_gr = o_gv + gh * dv
    o_lr = o_gr + gh * dv
    o_m = o_lr + 2 * rank
    assert s % MOD_BLK == 0 and sd % (2 * MOD_BLK) == 0 and sd % GRID_W == 0 and s % LANES == 0
    assert row_lat % sd == 0 and 2 * rank <= LANES and o_lr + LANES <= w_in.shape[-1]
    dims = dict(t=t, s=s, sd=sd, b=b, bd=bd, nkv=nkv, groups=groups, hd=hd, gh=gh, dk=dk, dv=dv, depth=depth,
                row_lat=row_lat, o_aq=o_aq, o_ak=o_ak, o_av=o_av, o_gq=o_gq, o_gk=o_gk, o_gv=o_gv, o_gr=o_gr,
                o_lr=o_lr)

    cap_c = EC_FACTOR * s // n_exp
    cap_l = EC_FACTOR * sd // n_exp
    mx = b * cap_c + bd * cap_l
    assert (b * cap_c) % cap_l == 0

    cond8 = jnp.zeros((8, d), F32).at[0].set(c_ctx).at[1:1 + bd].set(c)
    mod_all = _modulation(cond8, w_mod, b_mod)
    blk_row = np.concatenate([np.zeros(b * s // MOD_BLK, np.int32),
                              np.repeat(np.arange(1, bd + 1, dtype=np.int32), sd // MOD_BLK)])

    x_ctx = x_prompt.reshape(b * s, d)
    x_lat = x_sample.reshape(bd * sd, d)
    cos, sin = _rope_tables(sd, hd)
    w2p = jnp.zeros((depth, 2, LANES, gh * dk), F32)
    w2p = w2p.at[:, 0, :rank].set(gla_w2[:, 0]).at[:, 1, rank:2 * rank].set(gla_w2[:, 1])
    wr_pad = jnp.zeros((depth, d, LANES), F32).at[:, :, :n_exp].set(w_router)
    q_norm3 = q_norm.reshape(depth, 1, hd)
    k_norm3 = k_norm.reshape(depth, 1, hd)
    gla_norm3 = gla_norm.reshape(depth, 1, dv)
    cache_k4 = cache_k.reshape(bd, depth, cache_k.shape[2], nkv * hd)
    cache_v4 = cache_v.reshape(bd, depth, cache_v.shape[2], nkv * hd)
    w_in_t = jnp.swapaxes(w_in, 1, 2)
    w_conv_out_b, w_att_out_b, w_gla_out_b = (w.astype(BF16) for w in (w_conv_out, w_att_out, w_gla_out))

    new_k, new_v = [], []
    sf = sb = None
    for l in range(depth):
        modrows = mod_all[l][blk_row][:, None, :]
        hs = _normmod(x_ctx, 0, t, modrows, norm1[l], 0, d)
        (h,) = _normmod(x_lat, row_lat, t, modrows, norm1[l], 0, d, prev=hs)
        p = _proj(h, w_in_t, l, 0, o_lr, F32, _epi_identity, "in_proj")
        plr = _proj(h, w_in_t, l, o_lr, LANES, F32, _epi_identity, "lr_proj", tn=LANES)

        y_conv = _conv(p, conv_w, l, t, cw, s, sd, row_lat)

        y_att, k_new = _attn_ctx(p, q_norm3, k_norm3, l, dims)
        y_att = _attn_lat(p, q_norm3, k_norm3, cos, sin, cache_k4, cache_v4, y_att, l, dims)
        new_k.append(k_new.reshape(b, s, nkv, hd))
        new_v.append(p[:row_lat, o_av:o_av + nkv * hd].reshape(b, s, nkv, hd))

        y_gla, sf, sb = _gla_ctx(p, plr, w2p, gla_b, gla_norm3, sf, sb, l, dims)
        y_gla = _gla_lat(p, plr, w2p, gla_b, gla_norm3, state_gla_fwd, state_gla_bwd, y_gla, l, dims)

        gates = _proj(h, w_in_t, l, o_m, 3 * d, BF16, _epi_sigmoid, "gate_proj")
        merged = _merge(y_conv, y_att, y_gla, gates, w_conv_out_b, w_att_out_b, w_gla_out_b, l, d)
        x1_ctx = _out_proj(merged, w_o, l, x_ctx, 0, modrows, 2 * d)
        x1_lat = _out_proj(merged, w_o, l, x_lat, row_lat, modrows, 2 * d)

        hl = _normmod(x1_ctx, 0, t, modrows, norm2[l], 3 * d, 4 * d, w_router_pad=wr_pad[l])
        h2, logits = _normmod(x1_lat, row_lat, t, modrows, norm2[l], 3 * d, 4 * d, w_router_pad=wr_pad[l], prev=hl)
        xs, g_rows, pg_c = _router(logits, h2, s, b, 0, 0, mx, n_exp)
        xs, g_rows, pg_l = _router(logits, h2, sd, bd, row_lat, b * cap_c, mx, n_exp, prev=(xs, g_rows))
        hid = _ffn_up(xs, w_gate, w_up, l)
        ye = _ffn_down(hid, w_down, g_rows, l)
        x_ctx = _scatter(pg_c, ye, x1_ctx, modrows, 5 * d, s, 0, 0)
        x_lat = _scatter(pg_l, ye, x1_lat, modrows, 5 * d, sd, row_lat, b * cap_c)

    y_prompt = x_ctx.reshape(b, s, d)
    y_sample = x_lat.reshape(bd, sd, d)
    return (y_prompt, y_sample, jnp.stack(new_k, axis=1), jnp.stack(new_v, axis=1), sf, sb)
```

```python
import functools

import jax
import jax.numpy as jnp
import numpy as np
from jax import lax
from jax.experimental import pallas as pl
from jax.experimental.pallas import tpu as pltpu

F32 = jnp.float32
BF16 = jnp.bfloat16

GRID_W = 64
GLA_CHUNK = 64
GLA_TAU = 16.0
EC_FACTOR = 2
ROPE_THETA = 10000.0
EPS = 1e-6

LANES = 128
SUBLANES = 8
MOD_BLK = 256
GLA_GROUP = 256
NT_DIMS = (((1,), (1,)), ((), ()))
TN_DIMS = (((0,), (0,)), ((), ()))


def _params(sem, vmem_mb):
    return pltpu.CompilerParams(dimension_semantics=sem, vmem_limit_bytes=vmem_mb << 20)


def _any_spec():
    return pl.BlockSpec(memory_space=pl.ANY)


def _sigmoid(x):
    return 0.5 * jnp.tanh(0.5 * x) + 0.5


def _silu(x):
    return x * _sigmoid(x)


def _dot_split(a, b):
    a1 = a.astype(BF16)
    a2 = (a - a1.astype(F32)).astype(BF16)
    b1 = b.astype(BF16)
    b2 = (b - b1.astype(F32)).astype(BF16)
    return (jnp.dot(a1, b1, preferred_element_type=F32) + jnp.dot(a1, b2, preferred_element_type=F32)
            + jnp.dot(a2, b1, preferred_element_type=F32))


def _pick(total, candidates):
    for c in candidates:
        if total % c == 0:
            return c
    raise ValueError(f"no tile in {candidates} divides {total}")


def _mod_kernel(c_ref, w_ref, b_ref, o_ref):
    s = _silu(c_ref[...]).astype(BF16)
    o_ref[...] = jnp.dot(s, w_ref[...].astype(BF16), preferred_element_type=F32) + b_ref[...]


def _modulation(cond8, w_mod, b_mod):
    depth, d, n6 = w_mod.shape
    tn = _pick(n6, (512, 256, 128))
    return pl.pallas_call(
        _mod_kernel,
        out_shape=jax.ShapeDtypeStruct((depth, 8, n6), F32),
        grid=(depth, n6 // tn),
        in_specs=[pl.BlockSpec((8, d), lambda l, j: (0, 0)),
                  pl.BlockSpec((None, d, tn), lambda l, j: (l, 0, j)),
                  pl.BlockSpec((None, 1, tn), lambda l, j: (l, 0, j))],
        out_specs=pl.BlockSpec((None, 8, tn), lambda l, j: (l, 0, j)),
        compiler_params=_params(("arbitrary", "arbitrary"), 40),
        name="modulation",
    )(cond8, w_mod, b_mod.reshape(depth, 1, n6))


def _normmod_kernel(xc_ref, xl_ref, m_ref, g_ref, *rest, d, sh_off, sc_off, router, ctx_blocks):
    def emit(x_ref):
        x = x_ref[...]
        y = x * lax.rsqrt(jnp.mean(x * x, axis=-1, keepdims=True) + EPS) * g_ref[...]
        h = y * (1.0 + m_ref[:, sc_off:sc_off + d]) + m_ref[:, sh_off:sh_off + d]
        if router:
            wr_ref, o_ref, lg_ref = rest
            lg_ref[...] = _dot_split(h, wr_ref[...])
        else:
            (o_ref,) = rest
        o_ref[...] = h.astype(BF16)

    pl.when(pl.program_id(0) < ctx_blocks)(lambda: emit(xc_ref))
    pl.when(pl.program_id(0) >= ctx_blocks)(lambda: emit(xl_ref))


def _normmod(x_ctx, x_lat, modrows, gain, sh_off, sc_off, w_router_pad=None):
    (rc, d), rl = x_ctx.shape, x_lat.shape[0]
    router = w_router_pad is not None
    nc, nl = rc // MOD_BLK, rl // MOD_BLK
    in_specs = [pl.BlockSpec((MOD_BLK, d), lambda i: (jnp.minimum(i, nc - 1), 0)),
                pl.BlockSpec((MOD_BLK, d), lambda i: (jnp.maximum(i - nc, 0), 0)),
                pl.BlockSpec((None, 1, 6 * d), lambda i: (i, 0, 0)),
                pl.BlockSpec((1, d), lambda i: (0, 0))]
    args = [x_ctx, x_lat, modrows, gain.reshape(1, d)]
    out_shape = [jax.ShapeDtypeStruct((rc + rl, d), BF16)]
    out_specs = [pl.BlockSpec((MOD_BLK, d), lambda i: (i, 0))]
    if router:
        in_specs.append(pl.BlockSpec((d, LANES), lambda i: (0, 0)))
        args.append(w_router_pad)
        out_shape.append(jax.ShapeDtypeStruct((rc + rl, LANES), F32))
        out_specs.append(pl.BlockSpec((MOD_BLK, LANES), lambda i: (i, 0)))
    return pl.pallas_call(
        functools.partial(_normmod_kernel, d=d, sh_off=sh_off, sc_off=sc_off, router=router, ctx_blocks=nc),
        out_shape=out_shape, grid=(nc + nl,), in_specs=in_specs, out_specs=out_specs,
        compiler_params=_params(("arbitrary",), 48),
        name="normmod_router" if router else "normmod",
    )(*args)


def _mm_kernel(*refs, n_a, n_w, n_sides, w_rows, row_parts, epilogue):
    a_refs = refs[:n_a]
    w_refs = refs[n_a:n_a + n_w]
    side_refs = refs[n_a + n_w:n_a + n_w + n_sides]
    o_ref = refs[n_a + n_w + n_sides]
    wbs = [(w_ref[0] if w_rows else w_ref[...]).astype(BF16) for w_ref in w_refs]

    def dot(a, wb):
        if w_rows:
            return lax.dot_general(a, wb, NT_DIMS, preferred_element_type=F32)
        return jnp.dot(a, wb, preferred_element_type=F32)

    sides = [s[...] for s in side_refs]
    if row_parts:
        r0 = 0
        for a_ref in a_refs:
            r1 = r0 + a_ref.shape[0]
            o_ref[r0:r1, :] = epilogue([dot(a_ref[...], wb) for wb in wbs], sides).astype(o_ref.dtype)
            r0 = r1
    else:
        accs = [dot(a_refs[q if n_a > 1 else 0][...], wb) for q, wb in enumerate(wbs)]
        o_ref[...] = epilogue(accs, sides).astype(o_ref.dtype)


def _fused_mm(grid, a_args, w_args, side_args, out_shape, out_spec, epilogue, vmem_mb, name, w_rows=False,
              row_parts=False):
    assert row_parts or len(a_args) in (1, len(w_args))
    arrays = [a for a, _ in a_args] + [w for w, _ in w_args] + [s for s, _ in side_args]
    specs = [s for _, s in a_args] + [s for _, s in w_args] + [s for _, s in side_args]
    body = functools.partial(_mm_kernel, n_a=len(a_args), n_w=len(w_args), n_sides=len(side_args),
                             w_rows=w_rows, row_parts=row_parts, epilogue=epilogue)
    return pl.pallas_call(
        body, out_shape=out_shape, grid=grid, in_specs=specs, out_specs=out_spec,
        compiler_params=_params(("arbitrary",) * len(grid), vmem_mb), name=name,
    )(*arrays)


def _epi_identity(accs, sides):
    return accs[0]


def _epi_sigmoid(accs, sides):
    return _sigmoid(accs[0])


def _epi_residual(accs, sides):
    x, ga = sides
    return x + ga * accs[0]


def _epi_swiglu(accs, sides):
    return _silu(accs[0]) * accs[1]


def _epi_merge(accs, sides):
    return sides[0] * accs[0] + sides[1] * accs[1] + sides[2] * accs[2]


def _epi_rowscale(accs, sides):
    return accs[0] * jnp.concatenate(sides, axis=0)


def _proj(h, w_t, l, row0, n_cols, out_dtype, epilogue, name, tn=512):
    t, k = h.shape
    tm = _pick(t, (1536, 1024, 512, 256))
    assert row0 % SUBLANES == 0 and n_cols % tn == 0
    elem = (pl.Element(1), pl.Element(tn), pl.Element(k))
    return _fused_mm(
        (n_cols // tn, t // tm),
        [(h, pl.BlockSpec((tm, k), lambda j, i: (i, 0)))],
        [(w_t, pl.BlockSpec(elem, lambda j, i: (l, pl.multiple_of(row0 + j * tn, SUBLANES), 0)))],
        [], jax.ShapeDtypeStruct((t, n_cols), out_dtype),
        pl.BlockSpec((tm, tn), lambda j, i: (i, j)), epilogue, 58, name, w_rows=True)


def _merge(y_conv, y_att, y_gla, gates, w_conv_out, w_att_out, w_gla_out, l, d, row0):
    rows = y_att.shape[0]
    tm = _pick(rows, (1024, 512, 256))
    tn = _pick(d, (512, 256, 128))
    assert row0 % tm == 0
    ib0 = row0 // tm
    nb = d // tn
    ws = (w_conv_out, w_att_out, w_gla_out)
    a_args = [(y_conv, pl.BlockSpec((tm, y_conv.shape[1]), lambda j, i: (ib0 + i, 0))),
              (y_att, pl.BlockSpec((tm, y_att.shape[1]), lambda j, i: (i, 0))),
              (y_gla, pl.BlockSpec((tm, y_gla.shape[1]), lambda j, i: (i, 0)))]
    w_args = [(w, pl.BlockSpec((None, w.shape[1], tn), lambda j, i: (l, 0, j))) for w in ws]
    side_args = [(gates, pl.BlockSpec((tm, tn), functools.partial(lambda j, i, q: (ib0 + i, q * nb + j), q=q)))
                 for q in range(3)]
    return _fused_mm((nb, rows // tm), a_args, w_args, side_args,
                     jax.ShapeDtypeStruct((rows, d), BF16), pl.BlockSpec((tm, tn), lambda j, i: (i, j)),
                     _epi_merge, 58, "merge")


def _out_proj(merged, w_o, l, x_part, row0, modrows, ga_off):
    rows, d = x_part.shape
    tm = _pick(rows, (1024, 512, 256))
    tn = _pick(d, (512, 256, 128))
    mb0 = row0 // MOD_BLK
    rb = tm // MOD_BLK
    gb = ga_off // tn
    return _fused_mm(
        (d // tn, rows // tm),
        [(merged, pl.BlockSpec((tm, d), lambda j, i: (i, 0)))],
        [(w_o, pl.BlockSpec((None, d, tn), lambda j, i: (l, 0, j)))],
        [(x_part, pl.BlockSpec((tm, tn), lambda j, i: (i, j))),
         (modrows, pl.BlockSpec((None, 1, tn), lambda j, i: (mb0 + i * rb, 0, gb + j)))],
        jax.ShapeDtypeStruct((rows, d), F32), pl.BlockSpec((tm, tn), lambda j, i: (i, j)),
        _epi_residual, 52, "out_proj")


def _ffn_up(xs_parts, w_gate, w_up, l):
    e, _, d = xs_parts[0].shape
    mx = sum(x.shape[1] for x in xs_parts)
    f = w_gate.shape[-1]
    tf = _pick(f, (256, 128))
    wspec = pl.BlockSpec((None, None, d, tf), lambda ee, j: (l, ee, 0, j))
    return _fused_mm(
        (e, f // tf),
        [(x, pl.BlockSpec((None, x.shape[1], d), lambda ee, j: (ee, 0, 0))) for x in xs_parts],
        [(w_gate, wspec), (w_up, wspec)], [],
        jax.ShapeDtypeStruct((e, mx, f), BF16), pl.BlockSpec((None, mx, tf), lambda ee, j: (ee, 0, j)),
        _epi_swiglu, 58, "ffn_up", row_parts=True)


def _ffn_down(hid, w_down, gate_parts, l):
    e, mx, f = hid.shape
    d = w_down.shape[-1]
    tn = _pick(d, (1024, 512, 256, 128))
    return _fused_mm(
        (e, d // tn),
        [(hid, pl.BlockSpec((None, mx, f), lambda ee, j: (ee, 0, 0)))],
        [(w_down, pl.BlockSpec((None, None, f, tn), lambda ee, j: (l, ee, 0, j)))],
        [(g, pl.BlockSpec((None, g.shape[1], 1), lambda ee, j: (ee, 0, 0))) for g in gate_parts],
        jax.ShapeDtypeStruct((e, mx, d), BF16), pl.BlockSpec((None, mx, tn), lambda ee, j: (ee, 0, j)),
        _epi_rowscale, 56, "ffn_down")


def _conv_kernel(cb_ref, cc_ref, ch_ref, w_ref, o_ref, *, ctx_blocks, s, sd):
    u = cc_ref[...] * ch_ref[...]
    n = u.shape[0]
    row = lax.broadcasted_iota(jnp.int32, u.shape, 0)
    last = jnp.where(pl.program_id(0) < ctx_blocks, s - 1, sd - 1)
    pos = row & last
    prev = jnp.where(pos == 0, 0.0, pltpu.roll(u, 1, 0))
    nxt = jnp.where(pos == last, 0.0, pltpu.roll(u, n - 1, 0))
    z = w_ref[0:1, :] * prev + w_ref[1:2, :] * u + w_ref[2:3, :] * nxt
    o_ref[...] = (cb_ref[...] * z).astype(o_ref.dtype)


def _conv(p, conv_w, l, t, cw, s, sd, row_lat):
    assert s & (s - 1) == 0 and sd & (sd - 1) == 0 and sd % s == 0 and row_lat % sd == 0
    tc = _pick(cw, (512, 256, 128))
    nb = cw // tc
    return pl.pallas_call(
        functools.partial(_conv_kernel, ctx_blocks=row_lat // sd, s=s, sd=sd),
        out_shape=jax.ShapeDtypeStruct((t, cw), BF16), grid=(t // sd, nb),
        in_specs=[pl.BlockSpec((sd, tc), lambda b, j: (b, j)),
                  pl.BlockSpec((sd, tc), lambda b, j: (b, nb + j)),
                  pl.BlockSpec((sd, tc), lambda b, j: (b, 2 * nb + j)),
                  pl.BlockSpec((None, 3, tc), lambda b, j: (l, 0, j))],
        out_specs=pl.BlockSpec((sd, tc), lambda b, j: (b, j)),
        compiler_params=_params(("arbitrary", "arbitrary"), 40), name="conv",
    )(p, p, p, conv_w)


def _head_norm(x, gain):
    return x * lax.rsqrt(jnp.mean(x * x, axis=-1, keepdims=True) + EPS) * gain


def _rope(x, cos, sin_signed):
    hd = x.shape[-1]
    lane = lax.broadcasted_iota(jnp.int32, x.shape, 1)
    q = hd // 4
    swapped = jnp.where((lane & q) == 0, pltpu.roll(x, hd - q, 1), pltpu.roll(x, q, 1))
    return x * cos + swapped * sin_signed


def _attn_kernel(*refs, nkv, groups, hd, latent):
    if latent:
        (q_ref, k_ref, v_ref, qg_ref, kg_ref, cosq_ref, sinq_ref, cosk_ref, sink_ref,
         ck_ref, cv_ref, y_ref) = refs
    else:
        q_ref, k_ref, v_ref, qg_ref, kg_ref, y_ref, kout_ref = refs
    scale = hd ** -0.5
    for kv in range(nkv):
        ks = slice(kv * hd, (kv + 1) * hd)
        kn = _head_norm(k_ref[:, ks], kg_ref[...])
        if latent:
            kn = _rope(kn, cosk_ref[...], sink_ref[...])
            ckb = ck_ref[:, ks].astype(BF16)
            cvb = cv_ref[:, ks].astype(BF16)
        else:
            kout_ref[:, ks] = kn
        kb = kn.astype(BF16)
        vb = v_ref[:, ks].astype(BF16)
        for g in range(groups):
            qs = slice((kv * groups + g) * hd, (kv * groups + g + 1) * hd)
            qn = _head_norm(q_ref[:, qs], qg_ref[...])
            if latent:
                qn = _rope(qn, cosq_ref[...], sinq_ref[...])
            qb = qn.astype(BF16)
            s = lax.dot_general(qb, kb, NT_DIMS, preferred_element_type=F32) * scale
            m = jnp.max(s, axis=-1, keepdims=True)
            if latent:
                s2 = lax.dot_general(qb, ckb, NT_DIMS, preferred_element_type=F32) * scale
                m = jnp.maximum(m, jnp.max(s2, axis=-1, keepdims=True))
                p2 = jnp.exp(s2 - m)
            p = jnp.exp(s - m)
            den = jnp.sum(p, axis=-1, keepdims=True)
            o = jnp.dot(p.astype(BF16), vb, preferred_element_type=F32)
            if latent:
                den = den + jnp.sum(p2, axis=-1, keepdims=True)
                o = o + jnp.dot(p2.astype(BF16), cvb, preferred_element_type=F32)
            y_ref[:, qs] = (o / den).astype(y_ref.dtype)


def _attn_ctx(p, q_norm, k_norm, l, dims):
    t, n, nseq, nkv, groups, hd = dims["t"], dims["s"], dims["b"], dims["nkv"], dims["groups"], dims["hd"]
    aw, kw = nkv * groups * hd, nkv * hd
    assert dims["o_aq"] % aw == 0 and dims["o_ak"] % kw == 0 and dims["o_av"] % kw == 0
    qb0, kb0, vb0 = dims["o_aq"] // aw, dims["o_ak"] // kw, dims["o_av"] // kw
    return pl.pallas_call(
        functools.partial(_attn_kernel, nkv=nkv, groups=groups, hd=hd, latent=False),
        out_shape=[jax.ShapeDtypeStruct((nseq * n, aw), BF16), jax.ShapeDtypeStruct((nseq * n, kw), F32)],
        grid=(nseq,),
        in_specs=[pl.BlockSpec((n, aw), lambda b: (b, qb0)),
                  pl.BlockSpec((n, kw), lambda b: (b, kb0)),
                  pl.BlockSpec((n, kw), lambda b: (b, vb0)),
                  pl.BlockSpec((None, 1, hd), lambda b: (l, 0, 0)),
                  pl.BlockSpec((None, 1, hd), lambda b: (l, 0, 0))],
        out_specs=[pl.BlockSpec((n, aw), lambda b: (b, 0)),
                   pl.BlockSpec((n, kw), lambda b: (b, 0))],
        compiler_params=_params(("arbitrary",), 40), name="attn_ctx",
    )(p, p, p, q_norm, k_norm)


def _attn_lat(p, q_norm, k_norm, cos, sin, cache_k, cache_v, l, dims):
    t, n, nseq, nkv, groups, hd = dims["t"], dims["sd"], dims["bd"], dims["nkv"], dims["groups"], dims["hd"]
    past = cache_k.shape[2]
    aw, kw = nkv * groups * hd, nkv * hd
    tq = _pick(n, (256, 128))
    nq = n // tq
    qb0, kb0, vb0 = dims["o_aq"] // aw, dims["o_ak"] // kw, dims["o_av"] // kw
    rq0 = dims["row_lat"] // tq
    rk0 = dims["row_lat"] // n
    return pl.pallas_call(
        functools.partial(_attn_kernel, nkv=nkv, groups=groups, hd=hd, latent=True),
        out_shape=jax.ShapeDtypeStruct((nseq * n, aw), BF16),
        grid=(nseq, nq),
        in_specs=[pl.BlockSpec((tq, aw), lambda b, i: (rq0 + b * nq + i, qb0)),
                  pl.BlockSpec((n, kw), lambda b, i: (rk0 + b, kb0)),
                  pl.BlockSpec((n, kw), lambda b, i: (rk0 + b, vb0)),
                  pl.BlockSpec((None, 1, hd), lambda b, i: (l, 0, 0)),
                  pl.BlockSpec((None, 1, hd), lambda b, i: (l, 0, 0)),
                  pl.BlockSpec((tq, hd), lambda b, i: (i, 0)),
                  pl.BlockSpec((tq, hd), lambda b, i: (i, 0)),
                  pl.BlockSpec((n, hd), lambda b, i: (0, 0)),
                  pl.BlockSpec((n, hd), lambda b, i: (0, 0)),
                  pl.BlockSpec((None, None, past, kw), lambda b, i: (b, l, 0, 0)),
                  pl.BlockSpec((None, None, past, kw), lambda b, i: (b, l, 0, 0))],
        out_specs=pl.BlockSpec((tq, aw), lambda b, i: (b * nq + i, 0)),
        compiler_params=_params(("arbitrary",) * 2, 48), name="attn_lat",
    )(p, p, p, q_norm, k_norm, cos, sin, cos, sin, cache_k, cache_v)


def _log_sigmoid(z):
    return jnp.minimum(z, 0.0) - jnp.log1p(jnp.exp(-jnp.abs(z)))


def _dot01(m01, x):
    x1 = x.astype(BF16)
    r1 = x - x1.astype(F32)
    x2 = r1.astype(BF16)
    x3 = (r1 - x2.astype(F32)).astype(BF16)
    return (jnp.dot(m01, x1, preferred_element_type=F32) + jnp.dot(m01, x2, preferred_element_type=F32)
            + jnp.dot(m01, x3, preferred_element_type=F32))


def _gla_kernel(*refs, n, dk, dv, hps, latent, fill_slots):
    q_ref, k_ref, v_ref, r_ref, lr_ref, w2_ref, b_ref, gn_ref = refs[:8]
    stf_ref, stb_ref, of_ref, ob_ref, gf_ref, gb_ref = refs[-6:]
    if latent:
        sf0_ref, sb0_ref, y_ref = refs[8:11]
        sf_ref = sb_ref = None
    else:
        y_ref, sf_ref, sb_ref = refs[-9:-6]
        sf0_ref = sb0_ref = None
    L = GLA_CHUNK
    G = min(GLA_GROUP, n)
    cpg = G // L
    ng = n // G
    scale = dk ** -0.5

    lr = lr_ref[...]
    for d, g_ref in ((0, gf_ref), (1, gb_ref)):
        z = _dot_split(lr, w2_ref[d]) + b_ref[d:d + 1, :]
        g_ref[...] = _log_sigmoid(z) / GLA_TAU

    ri = lax.broadcasted_iota(jnp.int32, (G, G), 0)
    ci = lax.broadcasted_iota(jnp.int32, (G, G), 1)
    same_chunk = (ri // L) == (ci // L)

    def run(forward, hh, g_ref, s0_ref, s_out_ref, st_ref, o_ref):
        kc_ = slice(hh * dk, (hh + 1) * dk)
        vc_ = slice(hh * dv, (hh + 1) * dv)
        tri = same_chunk & ((ri >= ci) if forward else (ri <= ci))
        tri01 = jnp.where(tri, 1.0, 0.0).astype(BF16)
        st_ref[hh] = jnp.zeros((dv, dk), F32) if s0_ref is None else s0_ref[hh].T
        for gi in (range(ng) if forward else reversed(range(ng))):
            r0 = gi * G
            b = _dot01(tri01, g_ref[r0:r0 + G, kc_])
            b3 = b.reshape(cpg, L, dk)
            bl3 = b3[:, L - 1:L, :] if forward else b3[:, 0:1, :]
            bl = jnp.broadcast_to(bl3, (cpg, L, dk)).reshape(G, dk)
            kc = k_ref[r0:r0 + G, kc_]
            qe = (q_ref[r0:r0 + G, kc_] * scale * jnp.exp(b)).astype(BF16)
            ke = (kc * jnp.exp(-b)).astype(BF16)
            kd = (kc * jnp.exp(bl - b)).astype(BF16)
            vb = v_ref[r0:r0 + G, vc_].astype(BF16)
            a = lax.dot_general(qe, ke, NT_DIMS, preferred_element_type=F32)
            a = jnp.where(tri, a, 0.0).astype(BF16)
            o_intra = jnp.dot(a, vb, preferred_element_type=F32)
            for c in (range(cpg) if forward else reversed(range(cpg))):
                cs = slice(c * L, (c + 1) * L)
                st = st_ref[hh]
                o_ref[r0 + c * L:r0 + (c + 1) * L, vc_] = o_intra[cs] + lax.dot_general(
                    qe[cs], st.astype(BF16), NT_DIMS, preferred_element_type=F32)
                st_ref[hh] = st * jnp.exp(bl3[c]) + lax.dot_general(
                    vb[cs], kd[cs], TN_DIMS, preferred_element_type=F32)
        if s_out_ref is not None:
            if fill_slots:
                s_out_ref[0, hh] = st_ref[hh].T
                for slot in range(1, s_out_ref.shape[0]):
                    s_out_ref[slot, hh] = jnp.zeros((dk, dv), F32)
            else:
                s_out_ref[hh] = st_ref[hh].T

    for hh in range(hps):
        run(True, hh, gf_ref, sf0_ref, sf_ref, stf_ref, of_ref)
        run(False, hh, gb_ref, sb0_ref, sb_ref, stb_ref, ob_ref)

    for hh in range(hps):
        vc_ = slice(hh * dv, (hh + 1) * dv)
        o = of_ref[:, vc_] + ob_ref[:, vc_]
        on = o * lax.rsqrt(jnp.mean(o * o, axis=-1, keepdims=True) + EPS) * gn_ref[...]
        y_ref[:, vc_] = (on * _silu(r_ref[:, vc_])).astype(y_ref.dtype)


GLA_HEADS_PER_STEP = 2


def _gla_heads_per_step(dims):
    hps = GLA_HEADS_PER_STEP
    ok = dims["gh"] % hps == 0 and all(
        dims[o] % (hps * dims[w]) == 0 for o, w in (("o_gq", "dk"), ("o_gk", "dk"), ("o_gv", "dv"), ("o_gr", "dv")))
    return hps if ok else 1


def _gla_common_specs(l, dims, n, rb0, hps):
    dk, dv = dims["dk"] * hps, dims["dv"] * hps
    qb0, kb0, vb0, rb_ = dims["o_gq"] // dk, dims["o_gk"] // dk, dims["o_gv"] // dv, dims["o_gr"] // dv
    return [pl.BlockSpec((n, dk), lambda b, h: (rb0 + b, qb0 + h)),
            pl.BlockSpec((n, dk), lambda b, h: (rb0 + b, kb0 + h)),
            pl.BlockSpec((n, dv), lambda b, h: (rb0 + b, vb0 + h)),
            pl.BlockSpec((n, dv), lambda b, h: (rb0 + b, rb_ + h)),
            pl.BlockSpec((n, LANES), lambda b, h: (rb0 + b, 0)),
            pl.BlockSpec((None, 2, LANES, dk), lambda b, h: (l, 0, 0, h)),
            pl.BlockSpec((None, 2, dk), lambda b, h: (l, 0, h)),
            pl.BlockSpec((None, 1, dims["dv"]), lambda b, h: (l, 0, 0))]


def _gla_scratch(n, dk, dv, hps):
    return [pltpu.VMEM((hps, dv, dk), F32), pltpu.VMEM((hps, dv, dk), F32),
            pltpu.VMEM((n, hps * dv), F32), pltpu.VMEM((n, hps * dv), F32),
            pltpu.VMEM((n, hps * dk), F32), pltpu.VMEM((n, hps * dk), F32)]


def _gla_ctx(p, plr, w2p, gla_b, gla_norm, sf_prev, sb_prev, l, dims):
    n, nseq, gh, dk, dv, depth = dims["s"], dims["b"], dims["gh"], dims["dk"], dims["dv"], dims["depth"]
    hps = _gla_heads_per_step(dims)
    in_specs = _gla_common_specs(l, dims, n, 0, hps)
    args = [p, p, p, p, plr, w2p, gla_b, gla_norm]
    st_shape = jax.ShapeDtypeStruct((nseq, depth, gh, dk, dv), F32)
    first = sf_prev is None
    aliases = {}
    if first:
        st_spec = pl.BlockSpec((None, depth, hps, dk, dv), lambda b, h: (b, 0, h, 0, 0))
    else:
        in_specs += [_any_spec(), _any_spec()]
        args += [sf_prev, sb_prev]
        aliases = {8: 1, 9: 2}
        st_spec = pl.BlockSpec((None, None, hps, dk, dv), lambda b, h: (b, l, h, 0, 0))
    return pl.pallas_call(
        functools.partial(_gla_kernel, n=n, dk=dk, dv=dv, hps=hps, latent=False, fill_slots=first),
        out_shape=[jax.ShapeDtypeStruct((nseq * n, gh * dv), BF16), st_shape, st_shape],
        grid=(nseq, gh // hps), in_specs=in_specs,
        out_specs=[pl.BlockSpec((n, hps * dv), lambda b, h: (b, h)), st_spec, st_spec],
        scratch_shapes=_gla_scratch(n, dk, dv, hps), input_output_aliases=aliases,
        compiler_params=_params(("arbitrary", "arbitrary"), 40), name="gla_ctx",
    )(*args)


def _gla_lat(p, plr, w2p, gla_b, gla_norm, state_f, state_b, l, dims):
    n, nseq, gh, dk, dv = dims["sd"], dims["bd"], dims["gh"], dims["dk"], dims["dv"]
    hps = _gla_heads_per_step(dims)
    rb0 = dims["row_lat"] // n
    in_specs = _gla_common_specs(l, dims, n, rb0, hps)
    s_spec = pl.BlockSpec((None, None, hps, dk, dv), lambda b, h: (b, l, h, 0, 0))
    in_specs += [s_spec, s_spec]
    return pl.pallas_call(
        functools.partial(_gla_kernel, n=n, dk=dk, dv=dv, hps=hps, latent=True, fill_slots=False),
        out_shape=jax.ShapeDtypeStruct((nseq * n, gh * dv), BF16),
        grid=(nseq, gh // hps), in_specs=in_specs,
        out_specs=pl.BlockSpec((n, hps * dv), lambda b, h: (b, h)),
        scratch_shapes=_gla_scratch(n, dk, dv, hps),
        compiler_params=_params(("arbitrary", "arbitrary"), 56), name="gla_lat",
    )(p, p, p, p, plr, w2p, gla_b, gla_norm, state_f, state_b)


SEQ_TILE_ELEMS = 1 << 20


def _seq_col_tile(n, d):
    return _pick(d, tuple(c for c in (4096, 2048, 1024, 512, 256, 128) if n * c <= SEQ_TILE_ELEMS))


def _router_kernel(*refs, n, n_exp, cap):
    lg_ref, h_ref = refs[:2]
    xs_ref, g_ref, pg_ref, colb_ref, afft_ref = refs[-5:]
    nt = n // LANES

    @pl.when(pl.program_id(1) == 0)
    def _route():
        lg = lg_ref[...]
        lane = lax.broadcasted_iota(jnp.int32, lg.shape, 1)
        lg = jnp.where(lane < n_exp, lg, -jnp.inf)
        ex = jnp.exp(lg - jnp.max(lg, axis=-1, keepdims=True))
        aff = ex / jnp.sum(ex, axis=-1, keepdims=True)
        afft = aff.T
        for e in range(n_exp):
            afft_ref[e] = afft[e:e + 1, :]
            colb_ref[e] = jnp.broadcast_to(aff[:, e:e + 1], (n, LANES))
        si = lax.broadcasted_iota(jnp.int32, (LANES, LANES), 0)
        ti = lax.broadcasted_iota(jnp.int32, (LANES, LANES), 1)
        earlier = si < ti
        slot = lax.broadcasted_iota(jnp.int32, (cap, n), 0).astype(F32)

        def per_expert(e, carry):
            parts = []
            for tt in range(nt):
                row = afft_ref[e, :, tt * LANES:(tt + 1) * LANES]
                acc = jnp.zeros((LANES, LANES), F32)
                for c in range(nt):
                    col = colb_ref[e, c * LANES:(c + 1) * LANES, :]
                    if c < tt:
                        ahead = jnp.where(col >= row, 1.0, 0.0)
                    elif c > tt:
                        ahead = jnp.where(col > row, 1.0, 0.0)
                    else:
                        ahead = jnp.where(earlier, jnp.where(col >= row, 1.0, 0.0), jnp.where(col > row, 1.0, 0.0))
                    acc = acc + ahead
                parts.append(jnp.sum(acc, axis=0, keepdims=True))
            rank = jnp.concatenate(parts, axis=1)
            sel = rank == slot
            pg_ref[pl.ds(pl.multiple_of(e * cap, cap), cap), :] = jnp.where(sel, 1.0, 0.0).astype(BF16)
            g_ref[e] = jnp.sum(jnp.where(sel, afft_ref[e], 0.0), axis=1, keepdims=True)
            return carry

        lax.fori_loop(0, n_exp, per_expert, 0)

    xs = jnp.dot(pg_ref[...], h_ref[...], preferred_element_type=F32)
    for e in range(n_exp):
        xs_ref[e] = xs[e * cap:(e + 1) * cap, :].astype(xs_ref.dtype)


def _router(logits, h2, n, nseq, row0, n_exp):
    t, d = h2.shape
    cap = EC_FACTOR * n // n_exp
    td = _seq_col_tile(n, d)
    rb0 = row0 // n
    return pl.pallas_call(
        functools.partial(_router_kernel, n=n, n_exp=n_exp, cap=cap),
        out_shape=[jax.ShapeDtypeStruct((n_exp, nseq * cap, d), BF16),
                   jax.ShapeDtypeStruct((n_exp, nseq * cap, 1), F32),
                   jax.ShapeDtypeStruct((nseq, n_exp * cap, n), BF16)],
        grid=(nseq, d // td),
        in_specs=[pl.BlockSpec((n, LANES), lambda b, j: (rb0 + b, 0)),
                  pl.BlockSpec((n, td), lambda b, j: (rb0 + b, j))],
        out_specs=[pl.BlockSpec((n_exp, cap, td), lambda b, j: (0, b, j)),
                   pl.BlockSpec((n_exp, cap, 1), lambda b, j: (0, b, 0)),
                   pl.BlockSpec((None, n_exp * cap, n), lambda b, j: (b, 0, 0))],
        scratch_shapes=[pltpu.VMEM((n_exp, n, LANES), F32), pltpu.VMEM((n_exp, 1, n), F32)],
        compiler_params=_params(("arbitrary", "arbitrary"), 48), name="router",
    )(logits, h2)


def _scatter_kernel(pg_ref, ye_ref, x_ref, ga_ref, o_ref, *, n_exp):
    ye = jnp.concatenate([ye_ref[e] for e in range(n_exp)], axis=0)
    y = lax.dot_general(pg_ref[...], ye, TN_DIMS, preferred_element_type=F32)
    o_ref[...] = x_ref[...] + ga_ref[...] * y


def _scatter(pg, ye, x1_part, modrows, ga_off, n, row0, slot0):
    rows, d = x1_part.shape
    nseq = rows // n
    n_exp = ye.shape[0]
    cap = pg.shape[1] // n_exp
    td = _seq_col_tile(n, d)
    sb0 = slot0 // cap
    mb0 = row0 // MOD_BLK
    mstep = n // MOD_BLK
    gb = ga_off // td
    return pl.pallas_call(
        functools.partial(_scatter_kernel, n_exp=n_exp),
        out_shape=jax.ShapeDtypeStruct((rows, d), F32), grid=(nseq, d // td),
        in_specs=[pl.BlockSpec((None, n_exp * cap, n), lambda b, j: (b, 0, 0)),
                  pl.BlockSpec((n_exp, cap, td), lambda b, j: (0, sb0 + b, j)),
                  pl.BlockSpec((n, td), lambda b, j: (b, j)),
                  pl.BlockSpec((None, 1, td), lambda b, j: (mb0 + b * mstep, 0, gb + j))],
        out_specs=pl.BlockSpec((n, td), lambda b, j: (b, j)),
        compiler_params=_params(("arbitrary", "arbitrary"), 48), name="scatter",
    )(pg, ye, x1_part, modrows)


def _rope_tables(n, hd):
    qtr = hd // 4
    rows = n // GRID_W
    rpos = jnp.repeat(jnp.arange(rows, dtype=jnp.int32), GRID_W)
    cpos = jnp.arange(rows * GRID_W, dtype=jnp.int32) % GRID_W
    inv = ROPE_THETA ** (-jnp.arange(qtr, dtype=F32) / qtr)
    ar = rpos.astype(F32)[:, None] * inv[None, :]
    ac = cpos.astype(F32)[:, None] * inv[None, :]
    cos = jnp.concatenate([jnp.cos(ar), jnp.cos(ar), jnp.cos(ac), jnp.cos(ac)], axis=-1)
    sin = jnp.concatenate([-jnp.sin(ar), jnp.sin(ar), -jnp.sin(ac), jnp.sin(ac)], axis=-1)
    return cos, sin


def kernel(x_prompt, x_sample, cache_k, cache_v, state_gla_fwd, state_gla_bwd, c, c_ctx, w_mod, b_mod, norm1, norm2, w_in, conv_w, q_norm, k_norm, gla_w2, gla_b, gla_norm, w_conv_out, w_att_out, w_gla_out, w_o, w_router, w_gate, w_up, w_down):
    b, s, d = x_prompt.shape
    bd, sd, _ = x_sample.shape
    depth = w_mod.shape[0]
    cw = conv_w.shape[-1]
    hd = q_norm.shape[-1]
    nkv = cache_k.shape[3]
    aw = w_att_out.shape[1]
    groups = aw // hd // nkv
    gh, dk, dv = state_gla_fwd.shape[2:]
    rank = gla_w2.shape[2]
    n_exp = w_router.shape[-1]
    t = b * s + bd * sd
    row_lat = b * s

    o_aq = 3 * cw
    o_ak = o_aq + aw
    o_av = o_ak + nkv * hd
    o_gq = o_av + nkv * hd
    o_gk = o_gq + gh * dk
    o_gv = o_gk + gh * dk
    o_gr = o_gv + gh * dv
    o_lr = o_gr + gh * dv
    o_m = o_lr + 2 * rank
    assert s % MOD_BLK == 0 and sd % (2 * MOD_BLK) == 0 and sd % GRID_W == 0 and s % LANES == 0
    assert row_lat % sd == 0 and 2 * rank <= LANES and o_lr + LANES <= w_in.shape[-1]
    dims = dict(t=t, s=s, sd=sd, b=b, bd=bd, nkv=nkv, groups=groups, hd=hd, gh=gh, dk=dk, dv=dv, depth=depth,
                row_lat=row_lat, o_aq=o_aq, o_ak=o_ak, o_av=o_av, o_gq=o_gq, o_gk=o_gk, o_gv=o_gv, o_gr=o_gr,
                o_lr=o_lr)

    cap_c = EC_FACTOR * s // n_exp
    cap_l = EC_FACTOR * sd // n_exp
    mx = b * cap_c + bd * cap_l
    assert (b * cap_c) % cap_l == 0

    cond8 = jnp.zeros((8, d), F32).at[0].set(c_ctx).at[1:1 + bd].set(c)
    mod_all = _modulation(cond8, w_mod, b_mod)
    blk_row = np.concatenate([np.zeros(b * s // MOD_BLK, np.int32),
                              np.repeat(np.arange(1, bd + 1, dtype=np.int32), sd // MOD_BLK)])

    x_ctx = x_prompt.reshape(b * s, d)
    x_lat = x_sample.reshape(bd * sd, d)
    cos, sin = _rope_tables(sd, hd)
    w2p = jnp.zeros((depth, 2, LANES, gh * dk), F32)
    w2p = w2p.at[:, 0, :rank].set(gla_w2[:, 0]).at[:, 1, rank:2 * rank].set(gla_w2[:, 1])
    wr_pad = jnp.zeros((depth, d, LANES), F32).at[:, :, :n_exp].set(w_router)
    q_norm3 = q_norm.reshape(depth, 1, hd)
    k_norm3 = k_norm.reshape(depth, 1, hd)
    gla_norm3 = gla_norm.reshape(depth, 1, dv)
    cache_k4 = cache_k.reshape(bd, depth, cache_k.shape[2], nkv * hd)
    cache_v4 = cache_v.reshape(bd, depth, cache_v.shape[2], nkv * hd)
    w_in_t = jnp.swapaxes(w_in, 1, 2)
    w_conv_out_b, w_att_out_b, w_gla_out_b = (w.astype(BF16) for w in (w_conv_out, w_att_out, w_gla_out))

    new_k, new_v = [], []
    sf = sb = None
    for l in range(depth):
        modrows = mod_all[l][blk_row][:, None, :]
        (h,) = _normmod(x_ctx, x_lat, modrows, norm1[l], 0, d)
        p = _proj(h, w_in_t, l, 0, o_lr, F32, _epi_identity, "in_proj")
        plr = _proj(h, w_in_t, l, o_lr, LANES, F32, _epi_identity, "lr_proj", tn=LANES)

        y_conv = _conv(p, conv_w, l, t, cw, s, sd, row_lat)

        y_att_c, k_new = _attn_ctx(p, q_norm3, k_norm3, l, dims)
        y_att_l = _attn_lat(p, q_norm3, k_norm3, cos, sin, cache_k4, cache_v4, l, dims)
        new_k.append(k_new.reshape(b, s, nkv, hd))
        new_v.append(p[:row_lat, o_av:o_av + nkv * hd].reshape(b, s, nkv, hd))

        y_gla_c, sf, sb = _gla_ctx(p, plr, w2p, gla_b, gla_norm3, sf, sb, l, dims)
        y_gla_l = _gla_lat(p, plr, w2p, gla_b, gla_norm3, state_gla_fwd, state_gla_bwd, l, dims)

        gates = _proj(h, w_in_t, l, o_m, 3 * d, BF16, _epi_sigmoid, "gate_proj")
        out_w = (w_conv_out_b, w_att_out_b, w_gla_out_b)
        merged_c = _merge(y_conv, y_att_c, y_gla_c, gates, *out_w, l, d, 0)
        merged_l = _merge(y_conv, y_att_l, y_gla_l, gates, *out_w, l, d, row_lat)
        x1_ctx = _out_proj(merged_c, w_o, l, x_ctx, 0, modrows, 2 * d)
        x1_lat = _out_proj(merged_l, w_o, l, x_lat, row_lat, modrows, 2 * d)

        h2, logits = _normmod(x1_ctx, x1_lat, modrows, norm2[l], 3 * d, 4 * d, w_router_pad=wr_pad[l])
        xs_c, g_c, pg_c = _router(logits, h2, s, b, 0, n_exp)
        xs_l, g_l, pg_l = _router(logits, h2, sd, bd, row_lat, n_exp)
        hid = _ffn_up((xs_c, xs_l), w_gate, w_up, l)
        ye = _ffn_down(hid, w_down, (g_c, g_l), l)
        x_ctx = _scatter(pg_c, ye, x1_ctx, modrows, 5 * d, s, 0, 0)
        x_lat = _scatter(pg_l, ye, x1_lat, modrows, 5 * d, sd, row_lat, b * cap_c)

    y_prompt = x_ctx.reshape(b, s, d)
    y_sample = x_lat.reshape(bd, sd, d)
    return (y_prompt, y_sample, jnp.stack(new_k, axis=1), jnp.stack(new_v, axis=1), sf, sb)
```

```python
import functools

import jax
import jax.numpy as jnp
import numpy as np
from jax import lax
from jax.experimental import pallas as pl
from jax.experimental.pallas import tpu as pltpu

F32 = jnp.float32
BF16 = jnp.bfloat16

GRID_W = 64
GLA_CHUNK = 64
GLA_TAU = 16.0
EC_FACTOR = 2
ROPE_THETA = 10000.0
EPS = 1e-6
LOG2_E = 1.4426950408889634

LANES = 128
SUBLANES = 8
MOD_BLK = 256
GLA_GROUP = 256
NT_DIMS = (((1,), (1,)), ((), ()))
TN_DIMS = (((0,), (0,)), ((), ()))


def _params(sem, vmem_mb):
    return pltpu.CompilerParams(dimension_semantics=sem, vmem_limit_bytes=vmem_mb << 20)


def _any_spec():
    return pl.BlockSpec(memory_space=pl.ANY)


def _sigmoid(x):
    return 0.5 * jnp.tanh(0.5 * x) + 0.5


def _silu(x):
    return x * _sigmoid(x)


def _dot_split(a, b):
    a1 = a.astype(BF16)
    a2 = (a - a1.astype(F32)).astype(BF16)
    b1 = b.astype(BF16)
    b2 = (b - b1.astype(F32)).astype(BF16)
    return (jnp.dot(a1, b1, preferred_element_type=F32) + jnp.dot(a1, b2, preferred_element_type=F32)
            + jnp.dot(a2, b1, preferred_element_type=F32))


def _pick(total, candidates):
    for c in candidates:
        if total % c == 0:
            return c
    raise ValueError(f"no tile in {candidates} divides {total}")


def _mod_kernel(c_ref, w_ref, b_ref, o_ref):
    s = _silu(c_ref[...]).astype(BF16)
    o_ref[...] = jnp.dot(s, w_ref[...].astype(BF16), preferred_element_type=F32) + b_ref[...]


def _modulation(cond8, w_mod, b_mod):
    depth, d, n6 = w_mod.shape
    tn = _pick(n6, (512, 256, 128))
    return pl.pallas_call(
        _mod_kernel,
        out_shape=jax.ShapeDtypeStruct((depth, 8, n6), F32),
        grid=(depth, n6 // tn),
        in_specs=[pl.BlockSpec((8, d), lambda l, j: (0, 0)),
                  pl.BlockSpec((None, d, tn), lambda l, j: (l, 0, j)),
                  pl.BlockSpec((None, 1, tn), lambda l, j: (l, 0, j))],
        out_specs=pl.BlockSpec((None, 8, tn), lambda l, j: (l, 0, j)),
        compiler_params=_params(("arbitrary", "arbitrary"), 40),
        name="modulation",
    )(cond8, w_mod, b_mod.reshape(depth, 1, n6))


def _normmod_kernel(xc_ref, xl_ref, m_ref, g_ref, *rest, d, sh_off, sc_off, router, ctx_blocks):
    def emit(x_ref):
        x = x_ref[...]
        y = x * lax.rsqrt(jnp.mean(x * x, axis=-1, keepdims=True) + EPS) * g_ref[...]
        h = y * (1.0 + m_ref[:, sc_off:sc_off + d]) + m_ref[:, sh_off:sh_off + d]
        if router:
            wr_ref, o_ref, lg_ref = rest
            lg_ref[...] = _dot_split(h, wr_ref[...])
        else:
            (o_ref,) = rest
        o_ref[...] = h.astype(BF16)

    pl.when(pl.program_id(0) < ctx_blocks)(lambda: emit(xc_ref))
    pl.when(pl.program_id(0) >= ctx_blocks)(lambda: emit(xl_ref))


def _normmod(x_ctx, x_lat, modrows, gain, sh_off, sc_off, w_router_pad=None):
    (rc, d), rl = x_ctx.shape, x_lat.shape[0]
    router = w_router_pad is not None
    nc, nl = rc // MOD_BLK, rl // MOD_BLK
    in_specs = [pl.BlockSpec((MOD_BLK, d), lambda i: (jnp.minimum(i, nc - 1), 0)),
                pl.BlockSpec((MOD_BLK, d), lambda i: (jnp.maximum(i - nc, 0), 0)),
                pl.BlockSpec((None, 1, 6 * d), lambda i: (i, 0, 0)),
                pl.BlockSpec((1, d), lambda i: (0, 0))]
    args = [x_ctx, x_lat, modrows, gain.reshape(1, d)]
    out_shape = [jax.ShapeDtypeStruct((rc + rl, d), BF16)]
    out_specs = [pl.BlockSpec((MOD_BLK, d), lambda i: (i, 0))]
    if router:
        in_specs.append(pl.BlockSpec((d, LANES), lambda i: (0, 0)))
        args.append(w_router_pad)
        out_shape.append(jax.ShapeDtypeStruct((rc + rl, LANES), F32))
        out_specs.append(pl.BlockSpec((MOD_BLK, LANES), lambda i: (i, 0)))
    return pl.pallas_call(
        functools.partial(_normmod_kernel, d=d, sh_off=sh_off, sc_off=sc_off, router=router, ctx_blocks=nc),
        out_shape=out_shape, grid=(nc + nl,), in_specs=in_specs, out_specs=out_specs,
        compiler_params=_params(("arbitrary",), 48),
        name="normmod_router" if router else "normmod",
    )(*args)


def _mm_kernel(*refs, n_a, n_w, n_sides, w_rows, row_parts, epilogue):
    a_refs = refs[:n_a]
    w_refs = refs[n_a:n_a + n_w]
    side_refs = refs[n_a + n_w:n_a + n_w + n_sides]
    o_ref = refs[n_a + n_w + n_sides]
    wbs = [(w_ref[0] if w_rows else w_ref[...]).astype(BF16) for w_ref in w_refs]

    def dot(a, wb):
        if w_rows:
            return lax.dot_general(a, wb, NT_DIMS, preferred_element_type=F32)
        return jnp.dot(a, wb, preferred_element_type=F32)

    sides = [s[...] for s in side_refs]
    if row_parts:
        r0 = 0
        for a_ref in a_refs:
            r1 = r0 + a_ref.shape[0]
            o_ref[r0:r1, :] = epilogue([dot(a_ref[...], wb) for wb in wbs], sides).astype(o_ref.dtype)
            r0 = r1
    else:
        accs = [dot(a_refs[q if n_a > 1 else 0][...], wb) for q, wb in enumerate(wbs)]
        o_ref[...] = epilogue(accs, sides).astype(o_ref.dtype)


def _fused_mm(grid, a_args, w_args, side_args, out_shape, out_spec, epilogue, vmem_mb, name, w_rows=False,
              row_parts=False):
    assert row_parts or len(a_args) in (1, len(w_args))
    arrays = [a for a, _ in a_args] + [w for w, _ in w_args] + [s for s, _ in side_args]
    specs = [s for _, s in a_args] + [s for _, s in w_args] + [s for _, s in side_args]
    body = functools.partial(_mm_kernel, n_a=len(a_args), n_w=len(w_args), n_sides=len(side_args),
                             w_rows=w_rows, row_parts=row_parts, epilogue=epilogue)
    return pl.pallas_call(
        body, out_shape=out_shape, grid=grid, in_specs=specs, out_specs=out_spec,
        compiler_params=_params(("arbitrary",) * len(grid), vmem_mb), name=name,
    )(*arrays)


def _epi_identity(accs, sides):
    return accs[0]


def _epi_sigmoid(accs, sides):
    return _sigmoid(accs[0])


def _epi_residual(accs, sides):
    x, ga = sides
    return x + ga * accs[0]


def _epi_swiglu(accs, sides):
    return _silu(accs[0]) * accs[1]


def _epi_merge(accs, sides):
    return sides[0] * accs[0] + sides[1] * accs[1] + sides[2] * accs[2]


def _epi_rowscale(accs, sides):
    return accs[0] * jnp.concatenate(sides, axis=0)


def _proj(h, w_t, l, row0, n_cols, out_dtype, epilogue, name, tn=512):
    t, k = h.shape
    tm = _pick(t, (1536, 1024, 512, 256))
    assert row0 % SUBLANES == 0 and n_cols % tn == 0
    elem = (pl.Element(1), pl.Element(tn), pl.Element(k))
    return _fused_mm(
        (n_cols // tn, t // tm),
        [(h, pl.BlockSpec((tm, k), lambda j, i: (i, 0)))],
        [(w_t, pl.BlockSpec(elem, lambda j, i: (l, pl.multiple_of(row0 + j * tn, SUBLANES), 0)))],
        [], jax.ShapeDtypeStruct((t, n_cols), out_dtype),
        pl.BlockSpec((tm, tn), lambda j, i: (i, j)), epilogue, 58, name, w_rows=True)


def _merge(y_conv, y_att, y_gla, gates, w_conv_out, w_att_out, w_gla_out, l, d, row0):
    rows = y_att.shape[0]
    tm = _pick(rows, (1024, 512, 256))
    tn = _pick(d, (512, 256, 128))
    assert row0 % tm == 0
    ib0 = row0 // tm
    nb = d // tn
    ws = (w_conv_out, w_att_out, w_gla_out)
    a_args = [(y_conv, pl.BlockSpec((tm, y_conv.shape[1]), lambda j, i: (ib0 + i, 0))),
              (y_att, pl.BlockSpec((tm, y_att.shape[1]), lambda j, i: (i, 0))),
              (y_gla, pl.BlockSpec((tm, y_gla.shape[1]), lambda j, i: (i, 0)))]
    w_args = [(w, pl.BlockSpec((None, w.shape[1], tn), lambda j, i: (l, 0, j))) for w in ws]
    side_args = [(gates, pl.BlockSpec((tm, tn), functools.partial(lambda j, i, q: (ib0 + i, q * nb + j), q=q)))
                 for q in range(3)]
    return _fused_mm((nb, rows // tm), a_args, w_args, side_args,
                     jax.ShapeDtypeStruct((rows, d), BF16), pl.BlockSpec((tm, tn), lambda j, i: (i, j)),
                     _epi_merge, 58, "merge")


def _out_proj(merged, w_o, l, x_part, row0, modrows, ga_off):
    rows, d = x_part.shape
    tm = _pick(rows, (1024, 512, 256))
    tn = _pick(d, (512, 256, 128))
    mb0 = row0 // MOD_BLK
    rb = tm // MOD_BLK
    gb = ga_off // tn
    return _fused_mm(
        (d // tn, rows // tm),
        [(merged, pl.BlockSpec((tm, d), lambda j, i: (i, 0)))],
        [(w_o, pl.BlockSpec((None, d, tn), lambda j, i: (l, 0, j)))],
        [(x_part, pl.BlockSpec((tm, tn), lambda j, i: (i, j))),
         (modrows, pl.BlockSpec((None, 1, tn), lambda j, i: (mb0 + i * rb, 0, gb + j)))],
        jax.ShapeDtypeStruct((rows, d), F32), pl.BlockSpec((tm, tn), lambda j, i: (i, j)),
        _epi_residual, 52, "out_proj")


def _ffn_up(xs_parts, w_gate, w_up, l):
    e, _, d = xs_parts[0].shape
    mx = sum(x.shape[1] for x in xs_parts)
    f = w_gate.shape[-1]
    tf = _pick(f, (256, 128))
    wspec = pl.BlockSpec((None, None, d, tf), lambda ee, j: (l, ee, 0, j))
    return _fused_mm(
        (e, f // tf),
        [(x, pl.BlockSpec((None, x.shape[1], d), lambda ee, j: (ee, 0, 0))) for x in xs_parts],
        [(w_gate, wspec), (w_up, wspec)], [],
        jax.ShapeDtypeStruct((e, mx, f), BF16), pl.BlockSpec((None, mx, tf), lambda ee, j: (ee, 0, j)),
        _epi_swiglu, 58, "ffn_up", row_parts=True)


def _ffn_down(hid, w_down, gate_parts, l):
    e, mx, f = hid.shape
    d = w_down.shape[-1]
    tn = _pick(d, (1024, 512, 256, 128))
    return _fused_mm(
        (e, d // tn),
        [(hid, pl.BlockSpec((None, mx, f), lambda ee, j: (ee, 0, 0)))],
        [(w_down, pl.BlockSpec((None, None, f, tn), lambda ee, j: (l, ee, 0, j)))],
        [(g, pl.BlockSpec((None, g.shape[1], 1), lambda ee, j: (ee, 0, 0))) for g in gate_parts],
        jax.ShapeDtypeStruct((e, mx, d), BF16), pl.BlockSpec((None, mx, tn), lambda ee, j: (ee, 0, j)),
        _epi_rowscale, 56, "ffn_down")


def _conv_kernel(cb_ref, cc_ref, ch_ref, w_ref, o_ref, *, ctx_blocks, s, sd):
    u = cc_ref[...].astype(F32) * ch_ref[...].astype(F32)
    n = u.shape[0]
    row = lax.broadcasted_iota(jnp.int32, u.shape, 0)
    last = jnp.where(pl.program_id(0) < ctx_blocks, s - 1, sd - 1)
    pos = row & last
    prev = jnp.where(pos == 0, 0.0, pltpu.roll(u, 1, 0))
    nxt = jnp.where(pos == last, 0.0, pltpu.roll(u, n - 1, 0))
    z = w_ref[0:1, :] * prev + w_ref[1:2, :] * u + w_ref[2:3, :] * nxt
    o_ref[...] = (cb_ref[...].astype(F32) * z).astype(o_ref.dtype)


def _conv(p, conv_w, l, t, cw, s, sd, row_lat):
    assert s & (s - 1) == 0 and sd & (sd - 1) == 0 and sd % s == 0 and row_lat % sd == 0
    tc = _pick(cw, (512, 256, 128))
    nb = cw // tc
    return pl.pallas_call(
        functools.partial(_conv_kernel, ctx_blocks=row_lat // sd, s=s, sd=sd),
        out_shape=jax.ShapeDtypeStruct((t, cw), BF16), grid=(t // sd, nb),
        in_specs=[pl.BlockSpec((sd, tc), lambda b, j: (b, j)),
                  pl.BlockSpec((sd, tc), lambda b, j: (b, nb + j)),
                  pl.BlockSpec((sd, tc), lambda b, j: (b, 2 * nb + j)),
                  pl.BlockSpec((None, 3, tc), lambda b, j: (l, 0, j))],
        out_specs=pl.BlockSpec((sd, tc), lambda b, j: (b, j)),
        compiler_params=_params(("arbitrary", "arbitrary"), 40), name="conv",
    )(p, p, p, conv_w)


def _head_norm(x, gain):
    return x * lax.rsqrt(jnp.mean(x * x, axis=-1, keepdims=True) + EPS) * gain


def _rope(x, cos, sin_signed):
    hd = x.shape[-1]
    lane = lax.broadcasted_iota(jnp.int32, x.shape, 1)
    q = hd // 4
    swapped = jnp.where((lane & q) == 0, pltpu.roll(x, hd - q, 1), pltpu.roll(x, q, 1))
    return x * cos + swapped * sin_signed


def _attn_kernel(*refs, nkv, groups, hd, latent):
    if latent:
        (q_ref, k_ref, v_ref, qg_ref, kg_ref, cosq_ref, sinq_ref, cosk_ref, sink_ref,
         ck_ref, cv_ref, y_ref, kb_ref) = refs

        @pl.when(pl.program_id(1) == 0)
        def _keys():
            for kv in range(nkv):
                ks = slice(kv * hd, (kv + 1) * hd)
                kn = _rope(_head_norm(k_ref[:, ks], kg_ref[...]), cosk_ref[...], sink_ref[...])
                kb_ref[:, ks] = kn.astype(BF16)
    else:
        q_ref, k_ref, v_ref, qg_ref, kg_ref, y_ref, kout_ref = refs
    c = (hd ** -0.5) * LOG2_E
    for kv in range(nkv):
        ks = slice(kv * hd, (kv + 1) * hd)
        if latent:
            kb = kb_ref[:, ks]
            ckb = ck_ref[:, ks].astype(BF16)
            cvb = cv_ref[:, ks].astype(BF16)
        else:
            kn = _head_norm(k_ref[:, ks], kg_ref[...])
            kout_ref[:, ks] = kn
            kb = kn.astype(BF16)
        vb = v_ref[:, ks].astype(BF16)
        for g in range(groups):
            qs = slice((kv * groups + g) * hd, (kv * groups + g + 1) * hd)
            qn = _head_norm(q_ref[:, qs], qg_ref[...])
            if latent:
                qn = _rope(qn, cosq_ref[...], sinq_ref[...])
            qb = qn.astype(BF16)
            s = lax.dot_general(qb, kb, NT_DIMS, preferred_element_type=F32)
            m = jnp.max(s, axis=-1, keepdims=True)
            if latent:
                s2 = lax.dot_general(qb, ckb, NT_DIMS, preferred_element_type=F32)
                m = jnp.maximum(m, jnp.max(s2, axis=-1, keepdims=True))
                p2 = jnp.exp2((s2 - m) * c)
            p = jnp.exp2((s - m) * c)
            den = jnp.sum(p, axis=-1, keepdims=True)
            o = jnp.dot(p.astype(BF16), vb, preferred_element_type=F32)
            if latent:
                den = den + jnp.sum(p2, axis=-1, keepdims=True)
                o = o + jnp.dot(p2.astype(BF16), cvb, preferred_element_type=F32)
            y_ref[:, qs] = (o / den).astype(y_ref.dtype)


def _attn_ctx(p, q_norm, k_norm, l, dims):
    t, n, nseq, nkv, groups, hd = dims["t"], dims["s"], dims["b"], dims["nkv"], dims["groups"], dims["hd"]
    aw, kw = nkv * groups * hd, nkv * hd
    assert dims["o_aq"] % aw == 0 and dims["o_ak"] % kw == 0 and dims["o_av"] % kw == 0
    qb0, kb0, vb0 = dims["o_aq"] // aw, dims["o_ak"] // kw, dims["o_av"] // kw
    return pl.pallas_call(
        functools.partial(_attn_kernel, nkv=nkv, groups=groups, hd=hd, latent=False),
        out_shape=[jax.ShapeDtypeStruct((nseq * n, aw), BF16), jax.ShapeDtypeStruct((nseq * n, kw), F32)],
        grid=(nseq,),
        in_specs=[pl.BlockSpec((n, aw), lambda b: (b, qb0)),
                  pl.BlockSpec((n, kw), lambda b: (b, kb0)),
                  pl.BlockSpec((n, kw), lambda b: (b, vb0)),
                  pl.BlockSpec((None, 1, hd), lambda b: (l, 0, 0)),
                  pl.BlockSpec((None, 1, hd), lambda b: (l, 0, 0))],
        out_specs=[pl.BlockSpec((n, aw), lambda b: (b, 0)),
                   pl.BlockSpec((n, kw), lambda b: (b, 0))],
        compiler_params=_params(("arbitrary",), 40), name="attn_ctx",
    )(p, p, p, q_norm, k_norm)


def _attn_lat(p, q_norm, k_norm, cos, sin, cache_k, cache_v, l, dims):
    t, n, nseq, nkv, groups, hd = dims["t"], dims["sd"], dims["bd"], dims["nkv"], dims["groups"], dims["hd"]
    past = cache_k.shape[2]
    aw, kw = nkv * groups * hd, nkv * hd
    tq = _pick(n, (256, 128))
    nq = n // tq
    qb0, kb0, vb0 = dims["o_aq"] // aw, dims["o_ak"] // kw, dims["o_av"] // kw
    rq0 = dims["row_lat"] // tq
    rk0 = dims["row_lat"] // n
    return pl.pallas_call(
        functools.partial(_attn_kernel, nkv=nkv, groups=groups, hd=hd, latent=True),
        out_shape=jax.ShapeDtypeStruct((nseq * n, aw), BF16),
        grid=(nseq, nq),
        in_specs=[pl.BlockSpec((tq, aw), lambda b, i: (rq0 + b * nq + i, qb0)),
                  pl.BlockSpec((n, kw), lambda b, i: (rk0 + b, kb0)),
                  pl.BlockSpec((n, kw), lambda b, i: (rk0 + b, vb0)),
                  pl.BlockSpec((None, 1, hd), lambda b, i: (l, 0, 0)),
                  pl.BlockSpec((None, 1, hd), lambda b, i: (l, 0, 0)),
                  pl.BlockSpec((tq, hd), lambda b, i: (i, 0)),
                  pl.BlockSpec((tq, hd), lambda b, i: (i, 0)),
                  pl.BlockSpec((n, hd), lambda b, i: (0, 0)),
                  pl.BlockSpec((n, hd), lambda b, i: (0, 0)),
                  pl.BlockSpec((None, None, past, kw), lambda b, i: (b, l, 0, 0)),
                  pl.BlockSpec((None, None, past, kw), lambda b, i: (b, l, 0, 0))],
        out_specs=pl.BlockSpec((tq, aw), lambda b, i: (b * nq + i, 0)),
        scratch_shapes=[pltpu.VMEM((n, kw), BF16)],
        compiler_params=_params(("arbitrary",) * 2, 48), name="attn_lat",
    )(p, p, p, q_norm, k_norm, cos, sin, cos, sin, cache_k, cache_v)


def _log_sigmoid(z):
    return jnp.minimum(z, 0.0) - jnp.log1p(jnp.exp(-jnp.abs(z)))


def _dot01(m01, x):
    x1 = x.astype(BF16)
    r1 = x - x1.astype(F32)
    x2 = r1.astype(BF16)
    x3 = (r1 - x2.astype(F32)).astype(BF16)
    return (jnp.dot(m01, x1, preferred_element_type=F32) + jnp.dot(m01, x2, preferred_element_type=F32)
            + jnp.dot(m01, x3, preferred_element_type=F32))


def _gla_kernel(*refs, n, dk, dv, hps, latent, fill_slots):
    q_ref, k_ref, v_ref, r_ref, lr_ref, w2_ref, b_ref, gn_ref = refs[:8]
    stf_ref, stb_ref, of_ref, ob_ref, gf_ref, gb_ref = refs[-6:]
    if latent:
        sf0_ref, sb0_ref, y_ref = refs[8:11]
        sf_ref = sb_ref = None
    else:
        y_ref, sf_ref, sb_ref = refs[-9:-6]
        sf0_ref = sb0_ref = None
    L = GLA_CHUNK
    G = min(GLA_GROUP, n)
    cpg = G // L
    ng = n // G
    scale = dk ** -0.5

    lr = lr_ref[...]
    for d, g_ref in ((0, gf_ref), (1, gb_ref)):
        z = _dot_split(lr, w2_ref[d]) + b_ref[d:d + 1, :]
        g_ref[...] = _log_sigmoid(z) / GLA_TAU

    ri = lax.broadcasted_iota(jnp.int32, (G, G), 0)
    ci = lax.broadcasted_iota(jnp.int32, (G, G), 1)
    same_chunk = (ri // L) == (ci // L)

    def run(forward, hh, g_ref, s0_ref, s_out_ref, st_ref, o_ref):
        kc_ = slice(hh * dk, (hh + 1) * dk)
        vc_ = slice(hh * dv, (hh + 1) * dv)
        tri = same_chunk & ((ri >= ci) if forward else (ri <= ci))
        tri01 = jnp.where(tri, 1.0, 0.0).astype(BF16)
        st_ref[hh] = jnp.zeros((dk, dv), F32) if s0_ref is None else s0_ref[hh]
        for gi in (range(ng) if forward else reversed(range(ng))):
            r0 = gi * G
            b = _dot01(tri01, g_ref[r0:r0 + G, kc_])
            b3 = b.reshape(cpg, L, dk)
            bl3 = b3[:, L - 1:L, :] if forward else b3[:, 0:1, :]
            bl = jnp.broadcast_to(bl3, (cpg, L, dk)).reshape(G, dk)
            bl_rows = jnp.concatenate([bl3[c] for c in range(cpg)] + [jnp.zeros((SUBLANES - cpg, dk), F32)], axis=0)
            decay_cols = jnp.exp(bl_rows.T)
            kc = k_ref[r0:r0 + G, kc_]
            qe = (q_ref[r0:r0 + G, kc_] * scale * jnp.exp(b)).astype(BF16)
            ke = (kc * jnp.exp(-b)).astype(BF16)
            kd = (kc * jnp.exp(bl - b)).astype(BF16)
            vb = v_ref[r0:r0 + G, vc_].astype(BF16)
            a = lax.dot_general(qe, ke, NT_DIMS, preferred_element_type=F32)
            a = jnp.where(tri, a, 0.0).astype(BF16)
            o_intra = jnp.dot(a, vb, preferred_element_type=F32)
            for c in (range(cpg) if forward else reversed(range(cpg))):
                cs = slice(c * L, (c + 1) * L)
                st = st_ref[hh]
                o_ref[r0 + c * L:r0 + (c + 1) * L, vc_] = o_intra[cs] + jnp.dot(
                    qe[cs], st.astype(BF16), preferred_element_type=F32)
                st_ref[hh] = st * decay_cols[:, c:c + 1] + lax.dot_general(
                    kd[cs], vb[cs], TN_DIMS, preferred_element_type=F32)
        if s_out_ref is not None:
            if fill_slots:
                s_out_ref[0, hh] = st_ref[hh]
                for slot in range(1, s_out_ref.shape[0]):
                    s_out_ref[slot, hh] = jnp.zeros((dk, dv), F32)
            else:
                s_out_ref[hh] = st_ref[hh]

    for hh in range(hps):
        run(True, hh, gf_ref, sf0_ref, sf_ref, stf_ref, of_ref)
        run(False, hh, gb_ref, sb0_ref, sb_ref, stb_ref, ob_ref)

    for hh in range(hps):
        vc_ = slice(hh * dv, (hh + 1) * dv)
        o = of_ref[:, vc_] + ob_ref[:, vc_]
        on = o * lax.rsqrt(jnp.mean(o * o, axis=-1, keepdims=True) + EPS) * gn_ref[...]
        y_ref[:, vc_] = (on * _silu(r_ref[:, vc_])).astype(y_ref.dtype)


GLA_HEADS_PER_STEP = 2


def _gla_heads_per_step(dims):
    hps = GLA_HEADS_PER_STEP
    ok = dims["gh"] % hps == 0 and all(
        dims[o] % (hps * dims[w]) == 0 for o, w in (("o_gq", "dk"), ("o_gk", "dk"), ("o_gv", "dv"), ("o_gr", "dv")))
    return hps if ok else 1


def _gla_common_specs(l, dims, n, rb0, hps):
    dk, dv = dims["dk"] * hps, dims["dv"] * hps
    qb0, kb0, vb0, rb_ = dims["o_gq"] // dk, dims["o_gk"] // dk, dims["o_gv"] // dv, dims["o_gr"] // dv
    return [pl.BlockSpec((n, dk), lambda b, h: (rb0 + b, qb0 + h)),
            pl.BlockSpec((n, dk), lambda b, h: (rb0 + b, kb0 + h)),
            pl.BlockSpec((n, dv), lambda b, h: (rb0 + b, vb0 + h)),
            pl.BlockSpec((n, dv), lambda b, h: (rb0 + b, rb_ + h)),
            pl.BlockSpec((n, LANES), lambda b, h: (rb0 + b, 0)),
            pl.BlockSpec((None, 2, LANES, dk), lambda b, h: (l, 0, 0, h)),
            pl.BlockSpec((None, 2, dk), lambda b, h: (l, 0, h)),
            pl.BlockSpec((None, 1, dims["dv"]), lambda b, h: (l, 0, 0))]


def _gla_scratch(n, dk, dv, hps):
    return [pltpu.VMEM((hps, dk, dv), F32), pltpu.VMEM((hps, dk, dv), F32),
            pltpu.VMEM((n, hps * dv), F32), pltpu.VMEM((n, hps * dv), F32),
            pltpu.VMEM((n, hps * dk), F32), pltpu.VMEM((n, hps * dk), F32)]


def _gla_ctx(p, plr, w2p, gla_b, gla_norm, sf_prev, sb_prev, l, dims):
    n, nseq, gh, dk, dv, depth = dims["s"], dims["b"], dims["gh"], dims["dk"], dims["dv"], dims["depth"]
    hps = _gla_heads_per_step(dims)
    in_specs = _gla_common_specs(l, dims, n, 0, hps)
    args = [p, p, p, p, plr, w2p, gla_b, gla_norm]
    st_shape = jax.ShapeDtypeStruct((nseq, depth, gh, dk, dv), F32)
    first = sf_prev is None
    aliases = {}
    if first:
        st_spec = pl.BlockSpec((None, depth, hps, dk, dv), lambda b, h: (b, 0, h, 0, 0))
    else:
        in_specs += [_any_spec(), _any_spec()]
        args += [sf_prev, sb_prev]
        aliases = {8: 1, 9: 2}
        st_spec = pl.BlockSpec((None, None, hps, dk, dv), lambda b, h: (b, l, h, 0, 0))
    return pl.pallas_call(
        functools.partial(_gla_kernel, n=n, dk=dk, dv=dv, hps=hps, latent=False, fill_slots=first),
        out_shape=[jax.ShapeDtypeStruct((nseq * n, gh * dv), BF16), st_shape, st_shape],
        grid=(nseq, gh // hps), in_specs=in_specs,
        out_specs=[pl.BlockSpec((n, hps * dv), lambda b, h: (b, h)), st_spec, st_spec],
        scratch_shapes=_gla_scratch(n, dk, dv, hps), input_output_aliases=aliases,
        compiler_params=_params(("arbitrary", "arbitrary"), 40), name="gla_ctx",
    )(*args)


def _gla_lat(p, plr, w2p, gla_b, gla_norm, state_f, state_b, l, dims):
    n, nseq, gh, dk, dv = dims["sd"], dims["bd"], dims["gh"], dims["dk"], dims["dv"]
    hps = _gla_heads_per_step(dims)
    rb0 = dims["row_lat"] // n
    in_specs = _gla_common_specs(l, dims, n, rb0, hps)
    s_spec = pl.BlockSpec((None, None, hps, dk, dv), lambda b, h: (b, l, h, 0, 0))
    in_specs += [s_spec, s_spec]
    return pl.pallas_call(
        functools.partial(_gla_kernel, n=n, dk=dk, dv=dv, hps=hps, latent=True, fill_slots=False),
        out_shape=jax.ShapeDtypeStruct((nseq * n, gh * dv), BF16),
        grid=(nseq, gh // hps), in_specs=in_specs,
        out_specs=pl.BlockSpec((n, hps * dv), lambda b, h: (b, h)),
        scratch_shapes=_gla_scratch(n, dk, dv, hps),
        compiler_params=_params(("arbitrary", "arbitrary"), 56), name="gla_lat",
    )(p, p, p, p, plr, w2p, gla_b, gla_norm, state_f, state_b)


SEQ_TILE_ELEMS = 1 << 20


def _seq_col_tile(n, d):
    return _pick(d, tuple(c for c in (4096, 2048, 1024, 512, 256, 128) if n * c <= SEQ_TILE_ELEMS))


def _router_kernel(*refs, n, n_exp, cap):
    lg_ref, h_ref = refs[:2]
    xs_ref, g_ref, pg_ref, colb_ref, afft_ref = refs[-5:]
    nt = n // LANES

    @pl.when(pl.program_id(1) == 0)
    def _route():
        lg = lg_ref[...]
        lane = lax.broadcasted_iota(jnp.int32, lg.shape, 1)
        lg = jnp.where(lane < n_exp, lg, -jnp.inf)
        ex = jnp.exp(lg - jnp.max(lg, axis=-1, keepdims=True))
        aff = ex / jnp.sum(ex, axis=-1, keepdims=True)
        afft = aff.T
        for e in range(n_exp):
            afft_ref[e] = afft[e:e + 1, :]
            colb_ref[e] = jnp.broadcast_to(aff[:, e:e + 1], (n, LANES))
        si = lax.broadcasted_iota(jnp.int32, (LANES, LANES), 0)
        ti = lax.broadcasted_iota(jnp.int32, (LANES, LANES), 1)
        earlier = si < ti
        slot = lax.broadcasted_iota(jnp.int32, (cap, n), 0).astype(F32)

        def per_expert(e, carry):
            parts = []
            for tt in range(nt):
                row = afft_ref[e, :, tt * LANES:(tt + 1) * LANES]
                acc = jnp.zeros((LANES, LANES), F32)
                for c in range(nt):
                    col = colb_ref[e, c * LANES:(c + 1) * LANES, :]
                    if c < tt:
                        ahead = jnp.where(col >= row, 1.0, 0.0)
                    elif c > tt:
                        ahead = jnp.where(col > row, 1.0, 0.0)
                    else:
                        ahead = jnp.where(earlier, jnp.where(col >= row, 1.0, 0.0), jnp.where(col > row, 1.0, 0.0))
                    acc = acc + ahead
                parts.append(jnp.sum(acc, axis=0, keepdims=True))
            rank = jnp.concatenate(parts, axis=1)
            sel = rank == slot
            pg_ref[pl.ds(pl.multiple_of(e * cap, cap), cap), :] = jnp.where(sel, 1.0, 0.0).astype(BF16)
            g_ref[e] = jnp.sum(jnp.where(sel, afft_ref[e], 0.0), axis=1, keepdims=True)
            return carry

        lax.fori_loop(0, n_exp, per_expert, 0)

    xs = jnp.dot(pg_ref[...], h_ref[...], preferred_element_type=F32)
    for e in range(n_exp):
        xs_ref[e] = xs[e * cap:(e + 1) * cap, :].astype(xs_ref.dtype)


def _router(logits, h2, n, nseq, row0, n_exp):
    t, d = h2.shape
    cap = EC_FACTOR * n // n_exp
    td = _seq_col_tile(n, d)
    rb0 = row0 // n
    return pl.pallas_call(
        functools.partial(_router_kernel, n=n, n_exp=n_exp, cap=cap),
        out_shape=[jax.ShapeDtypeStruct((n_exp, nseq * cap, d), BF16),
                   jax.ShapeDtypeStruct((n_exp, nseq * cap, 1), F32),
                   jax.ShapeDtypeStruct((nseq, n_exp * cap, n), BF16)],
        grid=(nseq, d // td),
        in_specs=[pl.BlockSpec((n, LANES), lambda b, j: (rb0 + b, 0)),
                  pl.BlockSpec((n, td), lambda b, j: (rb0 + b, j))],
        out_specs=[pl.BlockSpec((n_exp, cap, td), lambda b, j: (0, b, j)),
                   pl.BlockSpec((n_exp, cap, 1), lambda b, j: (0, b, 0)),
                   pl.BlockSpec((None, n_exp * cap, n), lambda b, j: (b, 0, 0))],
        scratch_shapes=[pltpu.VMEM((n_exp, n, LANES), F32), pltpu.VMEM((n_exp, 1, n), F32)],
        compiler_params=_params(("arbitrary", "arbitrary"), 48), name="router",
    )(logits, h2)


def _scatter_kernel(pg_ref, ye_ref, x_ref, ga_ref, o_ref, *, n_exp):
    ye = jnp.concatenate([ye_ref[e] for e in range(n_exp)], axis=0)
    y = lax.dot_general(pg_ref[...], ye, TN_DIMS, preferred_element_type=F32)
    o_ref[...] = x_ref[...] + ga_ref[...] * y


def _scatter(pg, ye, x1_part, modrows, ga_off, n, row0, slot0):
    rows, d = x1_part.shape
    nseq = rows // n
    n_exp = ye.shape[0]
    cap = pg.shape[1] // n_exp
    td = _seq_col_tile(n, d)
    sb0 = slot0 // cap
    mb0 = row0 // MOD_BLK
    mstep = n // MOD_BLK
    gb = ga_off // td
    return pl.pallas_call(
        functools.partial(_scatter_kernel, n_exp=n_exp),
        out_shape=jax.ShapeDtypeStruct((rows, d), F32), grid=(nseq, d // td),
        in_specs=[pl.BlockSpec((None, n_exp * cap, n), lambda b, j: (b, 0, 0)),
                  pl.BlockSpec((n_exp, cap, td), lambda b, j: (0, sb0 + b, j)),
                  pl.BlockSpec((n, td), lambda b, j: (b, j)),
                  pl.BlockSpec((None, 1, td), lambda b, j: (mb0 + b * mstep, 0, gb + j))],
        out_specs=pl.BlockSpec((n, td), lambda b, j: (b, j)),
        compiler_params=_params(("arbitrary", "arbitrary"), 48), name="scatter",
    )(pg, ye, x1_part, modrows)


def _rope_tables(n, hd):
    qtr = hd // 4
    rows = n // GRID_W
    rpos = jnp.repeat(jnp.arange(rows, dtype=jnp.int32), GRID_W)
    cpos = jnp.arange(rows * GRID_W, dtype=jnp.int32) % GRID_W
    inv = ROPE_THETA ** (-jnp.arange(qtr, dtype=F32) / qtr)
    ar = rpos.astype(F32)[:, None] * inv[None, :]
    ac = cpos.astype(F32)[:, None] * inv[None, :]
    cos = jnp.concatenate([jnp.cos(ar), jnp.cos(ar), jnp.cos(ac), jnp.cos(ac)], axis=-1)
    sin = jnp.concatenate([-jnp.sin(ar), jnp.sin(ar), -jnp.sin(ac), jnp.sin(ac)], axis=-1)
    return cos, sin


def kernel(x_prompt, x_sample, cache_k, cache_v, state_gla_fwd, state_gla_bwd, c, c_ctx, w_mod, b_mod, norm1, norm2, w_in, conv_w, q_norm, k_norm, gla_w2, gla_b, gla_norm, w_conv_out, w_att_out, w_gla_out, w_o, w_router, w_gate, w_up, w_down):
    b, s, d = x_prompt.shape
    bd, sd, _ = x_sample.shape
    depth = w_mod.shape[0]
    cw = conv_w.shape[-1]
    hd = q_norm.shape[-1]
    nkv = cache_k.shape[3]
    aw = w_att_out.shape[1]
    groups = aw // hd // nkv
    gh, dk, dv = state_gla_fwd.shape[2:]
    rank = gla_w2.shape[2]
    n_exp = w_router.shape[-1]
    t = b * s + bd * sd
    row_lat = b * s

    n_conv = 3 * cw
    o_aq = 0
    o_ak = o_aq + aw
    o_av = o_ak + nkv * hd
    o_gq = o_av + nkv * hd
    o_gk = o_gq + gh * dk
    o_gv = o_gk + gh * dk
    o_gr = o_gv + gh * dv
    n_mix = o_gr + gh * dv
    o_lr = n_conv + n_mix
    o_m = o_lr + 2 * rank
    assert s % MOD_BLK == 0 and sd % MOD_BLK == 0 and sd % GRID_W == 0 and s % LANES == 0
    assert row_lat % sd == 0 and 2 * rank <= LANES and o_lr + LANES <= w_in.shape[-1]
    dims = dict(t=t, s=s, sd=sd, b=b, bd=bd, nkv=nkv, groups=groups, hd=hd, gh=gh, dk=dk, dv=dv, depth=depth,
                row_lat=row_lat, o_aq=o_aq, o_ak=o_ak, o_av=o_av, o_gq=o_gq, o_gk=o_gk, o_gv=o_gv, o_gr=o_gr,
                o_lr=o_lr)

    cap_c = EC_FACTOR * s // n_exp
    cap_l = EC_FACTOR * sd // n_exp
    mx = b * cap_c + bd * cap_l
    assert (b * cap_c) % cap_l == 0

    cond8 = jnp.zeros((8, d), F32).at[0].set(c_ctx).at[1:1 + bd].set(c)
    mod_all = _modulation(cond8, w_mod, b_mod)
    blk_row = np.concatenate([np.zeros(b * s // MOD_BLK, np.int32),
                              np.repeat(np.arange(1, bd + 1, dtype=np.int32), sd // MOD_BLK)])

    x_ctx = x_prompt.reshape(b * s, d)
    x_lat = x_sample.reshape(bd * sd, d)
    cos, sin = _rope_tables(sd, hd)
    w2p = jnp.zeros((depth, 2, LANES, gh * dk), F32)
    w2p = w2p.at[:, 0, :rank].set(gla_w2[:, 0]).at[:, 1, rank:2 * rank].set(gla_w2[:, 1])
    wr_pad = jnp.zeros((depth, d, LANES), F32).at[:, :, :n_exp].set(w_router)
    q_norm3 = q_norm.reshape(depth, 1, hd)
    k_norm3 = k_norm.reshape(depth, 1, hd)
    gla_norm3 = gla_norm.reshape(depth, 1, dv)
    cache_k4 = cache_k.reshape(bd, depth, cache_k.shape[2], nkv * hd)
    cache_v4 = cache_v.reshape(bd, depth, cache_v.shape[2], nkv * hd)
    w_in_t = jnp.swapaxes(w_in, 1, 2)
    w_conv_out_b, w_att_out_b, w_gla_out_b = (w.astype(BF16) for w in (w_conv_out, w_att_out, w_gla_out))

    new_k, new_v = [], []
    sf = sb = None
    for l in range(depth):
        modrows = mod_all[l][blk_row][:, None, :]
        (h,) = _normmod(x_ctx, x_lat, modrows, norm1[l], 0, d)
        p_conv = _proj(h, w_in_t, l, 0, n_conv, BF16, _epi_identity, "conv_proj")
        p = _proj(h, w_in_t, l, n_conv, n_mix, F32, _epi_identity, "in_proj")
        plr = _proj(h, w_in_t, l, o_lr, LANES, F32, _epi_identity, "lr_proj", tn=LANES)

        y_conv = _conv(p_conv, conv_w, l, t, cw, s, sd, row_lat)

        y_att_c, k_new = _attn_ctx(p, q_norm3, k_norm3, l, dims)
        y_att_l = _attn_lat(p, q_norm3, k_norm3, cos, sin, cache_k4, cache_v4, l, dims)
        new_k.append(k_new.reshape(b, s, nkv, hd))
        new_v.append(p[:row_lat, o_av:o_av + nkv * hd].reshape(b, s, nkv, hd))

        y_gla_c, sf, sb = _gla_ctx(p, plr, w2p, gla_b, gla_norm3, sf, sb, l, dims)
        y_gla_l = _gla_lat(p, plr, w2p, gla_b, gla_norm3, state_gla_fwd, state_gla_bwd, l, dims)

        gates = _proj(h, w_in_t, l, o_m, 3 * d, BF16, _epi_sigmoid, "gate_proj")
        out_w = (w_conv_out_b, w_att_out_b, w_gla_out_b)
        merged_c = _merge(y_conv, y_att_c, y_gla_c, gates, *out_w, l, d, 0)
        merged_l = _merge(y_conv, y_att_l, y_gla_l, gates, *out_w, l, d, row_lat)
        x1_ctx = _out_proj(merged_c, w_o, l, x_ctx, 0, modrows, 2 * d)
        x1_lat = _out_proj(merged_l, w_o, l, x_lat, row_lat, modrows, 2 * d)

        h2, logits = _normmod(x1_ctx, x1_lat, modrows, norm2[l], 3 * d, 4 * d, w_router_pad=wr_pad[l])
        xs_c, g_c, pg_c = _router(logits, h2, s, b, 0, n_exp)
        xs_l, g_l, pg_l = _router(logits, h2, sd, bd, row_lat, n_exp)
        hid = _ffn_up((xs_c, xs_l), w_gate, w_up, l)
        ye = _ffn_down(hid, w_down, (g_c, g_l), l)
        x_ctx = _scatter(pg_c, ye, x1_ctx, modrows, 5 * d, s, 0, 0)
        x_lat = _scatter(pg_l, ye, x1_lat, modrows, 5 * d, sd, row_lat, b * cap_c)

    y_prompt = x_ctx.reshape(b, s, d)
    y_sample = x_lat.reshape(bd, sd, d)
    return (y_prompt, y_sample, jnp.stack(new_k, axis=1), jnp.stack(new_v, axis=1), sf, sb)
```

```python
import functools

import jax
import jax.numpy as jnp
import numpy as np
from jax import lax
from jax.experimental import pallas as pl
from jax.experimental.pallas import tpu as pltpu

F32 = jnp.float32
BF16 = jnp.bfloat16

GRID_W = 64
GLA_CHUNK = 64
GLA_TAU = 16.0
EC_FACTOR = 2
ROPE_THETA = 10000.0
EPS = 1e-6
LOG2_E = 1.4426950408889634

LANES = 128
SUBLANES = 8
MOD_BLK = 256
GLA_GROUP = 256
NT_DIMS = (((1,), (1,)), ((), ()))
TN_DIMS = (((0,), (0,)), ((), ()))


def _params(sem, vmem_mb):
    return pltpu.CompilerParams(dimension_semantics=sem, vmem_limit_bytes=vmem_mb << 20)


def _any_spec():
    return pl.BlockSpec(memory_space=pl.ANY)


def _sigmoid(x):
    return 0.5 * jnp.tanh(0.5 * x) + 0.5


def _silu(x):
    return x * _sigmoid(x)


def _dot_split(a, b):
    a1 = a.astype(BF16)
    a2 = (a - a1.astype(F32)).astype(BF16)
    b1 = b.astype(BF16)
    b2 = (b - b1.astype(F32)).astype(BF16)
    return (jnp.dot(a1, b1, preferred_element_type=F32) + jnp.dot(a1, b2, preferred_element_type=F32)
            + jnp.dot(a2, b1, preferred_element_type=F32))


def _pick(total, candidates):
    for c in candidates:
        if total % c == 0:
            return c
    raise ValueError(f"no tile in {candidates} divides {total}")


def _mod_kernel(c_ref, w_ref, b_ref, o_ref):
    s = _silu(c_ref[...]).astype(BF16)
    o_ref[...] = jnp.dot(s, w_ref[...].astype(BF16), preferred_element_type=F32) + b_ref[...]


def _modulation(cond8, w_mod, b_mod):
    depth, d, n6 = w_mod.shape
    tn = _pick(n6, (512, 256, 128))
    return pl.pallas_call(
        _mod_kernel,
        out_shape=jax.ShapeDtypeStruct((depth, 8, n6), F32),
        grid=(depth, n6 // tn),
        in_specs=[pl.BlockSpec((8, d), lambda l, j: (0, 0)),
                  pl.BlockSpec((None, d, tn), lambda l, j: (l, 0, j)),
                  pl.BlockSpec((None, 1, tn), lambda l, j: (l, 0, j))],
        out_specs=pl.BlockSpec((None, 8, tn), lambda l, j: (l, 0, j)),
        compiler_params=_params(("arbitrary", "arbitrary"), 40),
        name="modulation",
    )(cond8, w_mod, b_mod.reshape(depth, 1, n6))


def _normmod_kernel(xc_ref, xl_ref, m_ref, g_ref, *rest, d, sh_off, sc_off, router, ctx_blocks):
    def emit(x_ref):
        x = x_ref[...]
        y = x * lax.rsqrt(jnp.mean(x * x, axis=-1, keepdims=True) + EPS) * g_ref[...]
        h = y * (1.0 + m_ref[:, sc_off:sc_off + d]) + m_ref[:, sh_off:sh_off + d]
        if router:
            wr_ref, o_ref, lg_ref = rest
            lg_ref[...] = _dot_split(h, wr_ref[...])
        else:
            (o_ref,) = rest
        o_ref[...] = h.astype(BF16)

    pl.when(pl.program_id(0) < ctx_blocks)(lambda: emit(xc_ref))
    pl.when(pl.program_id(0) >= ctx_blocks)(lambda: emit(xl_ref))


def _normmod(x_ctx, x_lat, modrows, gain, sh_off, sc_off, w_router_pad=None):
    (rc, d), rl = x_ctx.shape, x_lat.shape[0]
    router = w_router_pad is not None
    nc, nl = rc // MOD_BLK, rl // MOD_BLK
    in_specs = [pl.BlockSpec((MOD_BLK, d), lambda i: (jnp.minimum(i, nc - 1), 0)),
                pl.BlockSpec((MOD_BLK, d), lambda i: (jnp.maximum(i - nc, 0), 0)),
                pl.BlockSpec((None, 1, 6 * d), lambda i: (i, 0, 0)),
                pl.BlockSpec((1, d), lambda i: (0, 0))]
    args = [x_ctx, x_lat, modrows, gain.reshape(1, d)]
    out_shape = [jax.ShapeDtypeStruct((rc + rl, d), BF16)]
    out_specs = [pl.BlockSpec((MOD_BLK, d), lambda i: (i, 0))]
    if router:
        in_specs.append(pl.BlockSpec((d, LANES), lambda i: (0, 0)))
        args.append(w_router_pad)
        out_shape.append(jax.ShapeDtypeStruct((rc + rl, LANES), F32))
        out_specs.append(pl.BlockSpec((MOD_BLK, LANES), lambda i: (i, 0)))
    return pl.pallas_call(
        functools.partial(_normmod_kernel, d=d, sh_off=sh_off, sc_off=sc_off, router=router, ctx_blocks=nc),
        out_shape=out_shape, grid=(nc + nl,), in_specs=in_specs, out_specs=out_specs,
        compiler_params=_params(("arbitrary",), 48),
        name="normmod_router" if router else "normmod",
    )(*args)


def _mm_kernel(*refs, n_a, n_w, n_sides, w_rows, row_parts, epilogue):
    a_refs = refs[:n_a]
    w_refs = refs[n_a:n_a + n_w]
    side_refs = refs[n_a + n_w:n_a + n_w + n_sides]
    o_ref = refs[n_a + n_w + n_sides]
    wbs = [(w_ref[0] if w_rows else w_ref[...]).astype(BF16) for w_ref in w_refs]

    def dot(a, wb):
        if w_rows:
            return lax.dot_general(a, wb, NT_DIMS, preferred_element_type=F32)
        return jnp.dot(a, wb, preferred_element_type=F32)

    sides = [s[...] for s in side_refs]
    if row_parts:
        r0 = 0
        for a_ref in a_refs:
            r1 = r0 + a_ref.shape[0]
            o_ref[r0:r1, :] = epilogue([dot(a_ref[...], wb) for wb in wbs], sides).astype(o_ref.dtype)
            r0 = r1
    else:
        accs = [dot(a_refs[q if n_a > 1 else 0][...], wb) for q, wb in enumerate(wbs)]
        o_ref[...] = epilogue(accs, sides).astype(o_ref.dtype)


def _fused_mm(grid, a_args, w_args, side_args, out_shape, out_spec, epilogue, vmem_mb, name, w_rows=False,
              row_parts=False):
    assert row_parts or len(a_args) in (1, len(w_args))
    arrays = [a for a, _ in a_args] + [w for w, _ in w_args] + [s for s, _ in side_args]
    specs = [s for _, s in a_args] + [s for _, s in w_args] + [s for _, s in side_args]
    body = functools.partial(_mm_kernel, n_a=len(a_args), n_w=len(w_args), n_sides=len(side_args),
                             w_rows=w_rows, row_parts=row_parts, epilogue=epilogue)
    return pl.pallas_call(
        body, out_shape=out_shape, grid=grid, in_specs=specs, out_specs=out_spec,
        compiler_params=_params(("arbitrary",) * len(grid), vmem_mb), name=name,
    )(*arrays)


def _epi_identity(accs, sides):
    return accs[0]


def _epi_sigmoid(accs, sides):
    return _sigmoid(accs[0])


def _epi_residual(accs, sides):
    x, ga = sides
    return x + ga * accs[0]


def _epi_swiglu(accs, sides):
    return _silu(accs[0]) * accs[1]


def _epi_merge(accs, sides):
    return sides[0] * accs[0] + sides[1] * accs[1] + sides[2] * accs[2]


def _epi_rowscale(accs, sides):
    return accs[0] * jnp.concatenate(sides, axis=0)


def _proj(h, w_t, l, row0, n_cols, out_dtype, epilogue, name, tn=512):
    t, k = h.shape
    tm = _pick(t, (1536, 1024, 512, 256))
    assert row0 % SUBLANES == 0 and n_cols % tn == 0
    elem = (pl.Element(1), pl.Element(tn), pl.Element(k))
    return _fused_mm(
        (t // tm, n_cols // tn),
        [(h, pl.BlockSpec((tm, k), lambda i, j: (i, 0)))],
        [(w_t, pl.BlockSpec(elem, lambda i, j: (l, pl.multiple_of(row0 + j * tn, SUBLANES), 0)))],
        [], jax.ShapeDtypeStruct((t, n_cols), out_dtype),
        pl.BlockSpec((tm, tn), lambda i, j: (i, j)), epilogue, 58, name, w_rows=True)


def _merge(y_conv, y_att, y_gla, gates, w_conv_out, w_att_out, w_gla_out, l, d, row0):
    rows = y_att.shape[0]
    tm = _pick(rows, (1024, 512, 256))
    tn = _pick(d, (512, 256, 128))
    assert row0 % tm == 0
    ib0 = row0 // tm
    nb = d // tn
    ws = (w_conv_out, w_att_out, w_gla_out)
    a_args = [(y_conv, pl.BlockSpec((tm, y_conv.shape[1]), lambda i, j: (ib0 + i, 0))),
              (y_att, pl.BlockSpec((tm, y_att.shape[1]), lambda i, j: (i, 0))),
              (y_gla, pl.BlockSpec((tm, y_gla.shape[1]), lambda i, j: (i, 0)))]
    w_args = [(w, pl.BlockSpec((None, w.shape[1], tn), lambda i, j: (l, 0, j))) for w in ws]
    side_args = [(gates, pl.BlockSpec((tm, tn), functools.partial(lambda i, j, q: (ib0 + i, q * nb + j), q=q)))
                 for q in range(3)]
    return _fused_mm((rows // tm, nb), a_args, w_args, side_args,
                     jax.ShapeDtypeStruct((rows, d), BF16), pl.BlockSpec((tm, tn), lambda i, j: (i, j)),
                     _epi_merge, 58, "merge")


def _out_proj(merged, w_o, l, x_part, row0, modrows, ga_off):
    rows, d = x_part.shape
    tm = _pick(rows, (1024, 512, 256))
    tn = _pick(d, (512, 256, 128))
    mb0 = row0 // MOD_BLK
    rb = tm // MOD_BLK
    gb = ga_off // tn
    return _fused_mm(
        (rows // tm, d // tn),
        [(merged, pl.BlockSpec((tm, d), lambda i, j: (i, 0)))],
        [(w_o, pl.BlockSpec((None, d, tn), lambda i, j: (l, 0, j)))],
        [(x_part, pl.BlockSpec((tm, tn), lambda i, j: (i, j))),
         (modrows, pl.BlockSpec((None, 1, tn), lambda i, j: (mb0 + i * rb, 0, gb + j)))],
        jax.ShapeDtypeStruct((rows, d), F32), pl.BlockSpec((tm, tn), lambda i, j: (i, j)),
        _epi_residual, 52, "out_proj")


def _ffn_up(xs_parts, w_gate, w_up, l):
    e, _, d = xs_parts[0].shape
    mx = sum(x.shape[1] for x in xs_parts)
    f = w_gate.shape[-1]
    tf = _pick(f, (256, 128))
    wspec = pl.BlockSpec((None, None, d, tf), lambda ee, j: (l, ee, 0, j))
    return _fused_mm(
        (e, f // tf),
        [(x, pl.BlockSpec((None, x.shape[1], d), lambda ee, j: (ee, 0, 0))) for x in xs_parts],
        [(w_gate, wspec), (w_up, wspec)], [],
        jax.ShapeDtypeStruct((e, mx, f), BF16), pl.BlockSpec((None, mx, tf), lambda ee, j: (ee, 0, j)),
        _epi_swiglu, 58, "ffn_up", row_parts=True)


def _ffn_down(hid, w_down, gate_parts, l):
    e, mx, f = hid.shape
    d = w_down.shape[-1]
    tn = _pick(d, (1024, 512, 256, 128))
    return _fused_mm(
        (e, d // tn),
        [(hid, pl.BlockSpec((None, mx, f), lambda ee, j: (ee, 0, 0)))],
        [(w_down, pl.BlockSpec((None, None, f, tn), lambda ee, j: (l, ee, 0, j)))],
        [(g, pl.BlockSpec((None, g.shape[1], 1), lambda ee, j: (ee, 0, 0))) for g in gate_parts],
        jax.ShapeDtypeStruct((e, mx, d), BF16), pl.BlockSpec((None, mx, tn), lambda ee, j: (ee, 0, j)),
        _epi_rowscale, 56, "ffn_down")


def _conv_kernel(cb_ref, cc_ref, ch_ref, w_ref, o_ref, *, ctx_blocks, s, sd):
    u = cc_ref[...].astype(F32) * ch_ref[...].astype(F32)
    n = u.shape[0]
    row = lax.broadcasted_iota(jnp.int32, u.shape, 0)
    last = jnp.where(pl.program_id(0) < ctx_blocks, s - 1, sd - 1)
    pos = row & last
    prev = jnp.where(pos == 0, 0.0, pltpu.roll(u, 1, 0))
    nxt = jnp.where(pos == last, 0.0, pltpu.roll(u, n - 1, 0))
    z = w_ref[0:1, :] * prev + w_ref[1:2, :] * u + w_ref[2:3, :] * nxt
    o_ref[...] = (cb_ref[...].astype(F32) * z).astype(o_ref.dtype)


def _conv(p, conv_w, l, t, cw, s, sd, row_lat):
    assert s & (s - 1) == 0 and sd & (sd - 1) == 0 and sd % s == 0 and row_lat % sd == 0
    tc = _pick(cw, (512, 256, 128))
    nb = cw // tc
    return pl.pallas_call(
        functools.partial(_conv_kernel, ctx_blocks=row_lat // sd, s=s, sd=sd),
        out_shape=jax.ShapeDtypeStruct((t, cw), BF16), grid=(t // sd, nb),
        in_specs=[pl.BlockSpec((sd, tc), lambda b, j: (b, j)),
                  pl.BlockSpec((sd, tc), lambda b, j: (b, nb + j)),
                  pl.BlockSpec((sd, tc), lambda b, j: (b, 2 * nb + j)),
                  pl.BlockSpec((None, 3, tc), lambda b, j: (l, 0, j))],
        out_specs=pl.BlockSpec((sd, tc), lambda b, j: (b, j)),
        compiler_params=_params(("arbitrary", "arbitrary"), 40), name="conv",
    )(p, p, p, conv_w)


def _head_norm(x, gain):
    return x * lax.rsqrt(jnp.mean(x * x, axis=-1, keepdims=True) + EPS) * gain


def _rope(x, cos, sin_signed):
    hd = x.shape[-1]
    lane = lax.broadcasted_iota(jnp.int32, x.shape, 1)
    q = hd // 4
    swapped = jnp.where((lane & q) == 0, pltpu.roll(x, hd - q, 1), pltpu.roll(x, q, 1))
    return x * cos + swapped * sin_signed


def _attn_kernel(*refs, nkv, groups, hd, latent):
    if latent:
        (q_ref, k_ref, v_ref, qg_ref, kg_ref, cosq_ref, sinq_ref, cosk_ref, sink_ref,
         ck_ref, cv_ref, y_ref, kb_ref) = refs

        @pl.when(pl.program_id(1) == 0)
        def _keys():
            for kv in range(nkv):
                ks = slice(kv * hd, (kv + 1) * hd)
                kn = _rope(_head_norm(k_ref[:, ks], kg_ref[...]), cosk_ref[...], sink_ref[...])
                kb_ref[:, ks] = kn.astype(BF16)
    else:
        q_ref, k_ref, v_ref, qg_ref, kg_ref, y_ref, kout_ref = refs
    c = (hd ** -0.5) * LOG2_E
    for kv in range(nkv):
        ks = slice(kv * hd, (kv + 1) * hd)
        if latent:
            kb = kb_ref[:, ks]
            ckb = ck_ref[:, ks].astype(BF16)
            cvb = cv_ref[:, ks].astype(BF16)
        else:
            kn = _head_norm(k_ref[:, ks], kg_ref[...])
            kout_ref[:, ks] = kn
            kb = kn.astype(BF16)
        vb = v_ref[:, ks].astype(BF16)
        for g in range(groups):
            qs = slice((kv * groups + g) * hd, (kv * groups + g + 1) * hd)
            qn = _head_norm(q_ref[:, qs], qg_ref[...])
            if latent:
                qn = _rope(qn, cosq_ref[...], sinq_ref[...])
            qb = qn.astype(BF16)
            s = lax.dot_general(qb, kb, NT_DIMS, preferred_element_type=F32)
            m = jnp.max(s, axis=-1, keepdims=True)
            if latent:
                s2 = lax.dot_general(qb, ckb, NT_DIMS, preferred_element_type=F32)
                m = jnp.maximum(m, jnp.max(s2, axis=-1, keepdims=True))
                p2 = jnp.exp2((s2 - m) * c)
            p = jnp.exp2((s - m) * c)
            den = jnp.sum(p, axis=-1, keepdims=True)
            o = jnp.dot(p.astype(BF16), vb, preferred_element_type=F32)
            if latent:
                den = den + jnp.sum(p2, axis=-1, keepdims=True)
                o = o + jnp.dot(p2.astype(BF16), cvb, preferred_element_type=F32)
            y_ref[:, qs] = (o / den).astype(y_ref.dtype)


def _attn_ctx(p, q_norm, k_norm, l, dims):
    t, n, nseq, nkv, groups, hd = dims["t"], dims["s"], dims["b"], dims["nkv"], dims["groups"], dims["hd"]
    aw, kw = nkv * groups * hd, nkv * hd
    assert dims["o_aq"] % aw == 0 and dims["o_ak"] % kw == 0 and dims["o_av"] % kw == 0
    qb0, kb0, vb0 = dims["o_aq"] // aw, dims["o_ak"] // kw, dims["o_av"] // kw
    return pl.pallas_call(
        functools.partial(_attn_kernel, nkv=nkv, groups=groups, hd=hd, latent=False),
        out_shape=[jax.ShapeDtypeStruct((nseq * n, aw), BF16), jax.ShapeDtypeStruct((nseq * n, kw), F32)],
        grid=(nseq,),
        in_specs=[pl.BlockSpec((n, aw), lambda b: (b, qb0)),
                  pl.BlockSpec((n, kw), lambda b: (b, kb0)),
                  pl.BlockSpec((n, kw), lambda b: (b, vb0)),
                  pl.BlockSpec((None, 1, hd), lambda b: (l, 0, 0)),
                  pl.BlockSpec((None, 1, hd), lambda b: (l, 0, 0))],
        out_specs=[pl.BlockSpec((n, aw), lambda b: (b, 0)),
                   pl.BlockSpec((n, kw), lambda b: (b, 0))],
        compiler_params=_params(("arbitrary",), 40), name="attn_ctx",
    )(p, p, p, q_norm, k_norm)


def _attn_lat(p, q_norm, k_norm, cos, sin, cache_k, cache_v, l, dims):
    t, n, nseq, nkv, groups, hd = dims["t"], dims["sd"], dims["bd"], dims["nkv"], dims["groups"], dims["hd"]
    past = cache_k.shape[2]
    aw, kw = nkv * groups * hd, nkv * hd
    tq = _pick(n, (256, 128))
    nq = n // tq
    qb0, kb0, vb0 = dims["o_aq"] // aw, dims["o_ak"] // kw, dims["o_av"] // kw
    rq0 = dims["row_lat"] // tq
    rk0 = dims["row_lat"] // n
    return pl.pallas_call(
        functools.partial(_attn_kernel, nkv=nkv, groups=groups, hd=hd, latent=True),
        out_shape=jax.ShapeDtypeStruct((nseq * n, aw), BF16),
        grid=(nseq, nq),
        in_specs=[pl.BlockSpec((tq, aw), lambda b, i: (rq0 + b * nq + i, qb0)),
                  pl.BlockSpec((n, kw), lambda b, i: (rk0 + b, kb0)),
                  pl.BlockSpec((n, kw), lambda b, i: (rk0 + b, vb0)),
                  pl.BlockSpec((None, 1, hd), lambda b, i: (l, 0, 0)),
                  pl.BlockSpec((None, 1, hd), lambda b, i: (l, 0, 0)),
                  pl.BlockSpec((tq, hd), lambda b, i: (i, 0)),
                  pl.BlockSpec((tq, hd), lambda b, i: (i, 0)),
                  pl.BlockSpec((n, hd), lambda b, i: (0, 0)),
                  pl.BlockSpec((n, hd), lambda b, i: (0, 0)),
                  pl.BlockSpec((None, None, past, kw), lambda b, i: (b, l, 0, 0)),
                  pl.BlockSpec((None, None, past, kw), lambda b, i: (b, l, 0, 0))],
        out_specs=pl.BlockSpec((tq, aw), lambda b, i: (b * nq + i, 0)),
        scratch_shapes=[pltpu.VMEM((n, kw), BF16)],
        compiler_params=_params(("arbitrary",) * 2, 48), name="attn_lat",
    )(p, p, p, q_norm, k_norm, cos, sin, cos, sin, cache_k, cache_v)


def _log_sigmoid(z):
    return jnp.minimum(z, 0.0) - jnp.log1p(jnp.exp(-jnp.abs(z)))


def _dot01(m01, x):
    x1 = x.astype(BF16)
    r1 = x - x1.astype(F32)
    x2 = r1.astype(BF16)
    x3 = (r1 - x2.astype(F32)).astype(BF16)
    return (jnp.dot(m01, x1, preferred_element_type=F32) + jnp.dot(m01, x2, preferred_element_type=F32)
            + jnp.dot(m01, x3, preferred_element_type=F32))


def _gla_kernel(*refs, n, dk, dv, hps, latent, fill_slots):
    q_ref, k_ref, v_ref, r_ref, lr_ref, w2_ref, b_ref, gn_ref = refs[:8]
    stf_ref, stb_ref, of_ref, ob_ref, gf_ref, gb_ref = refs[-6:]
    if latent:
        sf0_ref, sb0_ref, y_ref = refs[8:11]
        sf_ref = sb_ref = None
    else:
        y_ref, sf_ref, sb_ref = refs[-9:-6]
        sf0_ref = sb0_ref = None
    L = GLA_CHUNK
    G = min(GLA_GROUP, n)
    cpg = G // L
    ng = n // G
    scale = dk ** -0.5

    lr = lr_ref[...]
    for d, g_ref in ((0, gf_ref), (1, gb_ref)):
        z = _dot_split(lr, w2_ref[d]) + b_ref[d:d + 1, :]
        g_ref[...] = _log_sigmoid(z) / GLA_TAU

    ri = lax.broadcasted_iota(jnp.int32, (G, G), 0)
    ci = lax.broadcasted_iota(jnp.int32, (G, G), 1)
    same_chunk = (ri // L) == (ci // L)

    def run(forward, hh, g_ref, s0_ref, s_out_ref, st_ref, o_ref):
        kc_ = slice(hh * dk, (hh + 1) * dk)
        vc_ = slice(hh * dv, (hh + 1) * dv)
        tri = same_chunk & ((ri >= ci) if forward else (ri <= ci))
        tri01 = jnp.where(tri, 1.0, 0.0).astype(BF16)
        st_ref[hh] = jnp.zeros((dk, dv), F32) if s0_ref is None else s0_ref[hh]
        for gi in (range(ng) if forward else reversed(range(ng))):
            r0 = gi * G
            b = _dot01(tri01, g_ref[r0:r0 + G, kc_])
            b3 = b.reshape(cpg, L, dk)
            bl3 = b3[:, L - 1:L, :] if forward else b3[:, 0:1, :]
            bl = jnp.broadcast_to(bl3, (cpg, L, dk)).reshape(G, dk)
            bl_rows = jnp.concatenate([bl3[c] for c in range(cpg)] + [jnp.zeros((SUBLANES - cpg, dk), F32)], axis=0)
            decay_cols = jnp.exp(bl_rows.T)
            kc = k_ref[r0:r0 + G, kc_]
            qe = (q_ref[r0:r0 + G, kc_] * scale * jnp.exp(b)).astype(BF16)
            ke = (kc * jnp.exp(-b)).astype(BF16)
            kd = (kc * jnp.exp(bl - b)).astype(BF16)
            vb = v_ref[r0:r0 + G, vc_].astype(BF16)
            a = lax.dot_general(qe, ke, NT_DIMS, preferred_element_type=F32)
            a = jnp.where(tri, a, 0.0).astype(BF16)
            o_intra = jnp.dot(a, vb, preferred_element_type=F32)
            for c in (range(cpg) if forward else reversed(range(cpg))):
                cs = slice(c * L, (c + 1) * L)
                st = st_ref[hh]
                o_ref[r0 + c * L:r0 + (c + 1) * L, vc_] = o_intra[cs] + jnp.dot(
                    qe[cs], st.astype(BF16), preferred_element_type=F32)
                st_ref[hh] = st * decay_cols[:, c:c + 1] + lax.dot_general(
                    kd[cs], vb[cs], TN_DIMS, preferred_element_type=F32)
        if s_out_ref is not None:
            if fill_slots:
                s_out_ref[0, hh] = st_ref[hh]
                for slot in range(1, s_out_ref.shape[0]):
                    s_out_ref[slot, hh] = jnp.zeros((dk, dv), F32)
            else:
                s_out_ref[hh] = st_ref[hh]

    for hh in range(hps):
        run(True, hh, gf_ref, sf0_ref, sf_ref, stf_ref, of_ref)
        run(False, hh, gb_ref, sb0_ref, sb_ref, stb_ref, ob_ref)

    for hh in range(hps):
        vc_ = slice(hh * dv, (hh + 1) * dv)
        o = of_ref[:, vc_] + ob_ref[:, vc_]
        on = o * lax.rsqrt(jnp.mean(o * o, axis=-1, keepdims=True) + EPS) * gn_ref[...]
        y_ref[:, vc_] = (on * _silu(r_ref[:, vc_])).astype(y_ref.dtype)


GLA_HEADS_PER_STEP = 2


def _gla_heads_per_step(dims):
    hps = GLA_HEADS_PER_STEP
    ok = dims["gh"] % hps == 0 and all(
        dims[o] % (hps * dims[w]) == 0 for o, w in (("o_gq", "dk"), ("o_gk", "dk"), ("o_gv", "dv"), ("o_gr", "dv")))
    return hps if ok else 1


def _gla_common_specs(l, dims, n, rb0, hps):
    dk, dv = dims["dk"] * hps, dims["dv"] * hps
    qb0, kb0, vb0, rb_ = dims["o_gq"] // dk, dims["o_gk"] // dk, dims["o_gv"] // dv, dims["o_gr"] // dv
    return [pl.BlockSpec((n, dk), lambda b, h: (rb0 + b, qb0 + h)),
            pl.BlockSpec((n, dk), lambda b, h: (rb0 + b, kb0 + h)),
            pl.BlockSpec((n, dv), lambda b, h: (rb0 + b, vb0 + h)),
            pl.BlockSpec((n, dv), lambda b, h: (rb0 + b, rb_ + h)),
            pl.BlockSpec((n, LANES), lambda b, h: (rb0 + b, 0)),
            pl.BlockSpec((None, 2, LANES, dk), lambda b, h: (l, 0, 0, h)),
            pl.BlockSpec((None, 2, dk), lambda b, h: (l, 0, h)),
            pl.BlockSpec((None, 1, dims["dv"]), lambda b, h: (l, 0, 0))]


def _gla_scratch(n, dk, dv, hps):
    return [pltpu.VMEM((hps, dk, dv), F32), pltpu.VMEM((hps, dk, dv), F32),
            pltpu.VMEM((n, hps * dv), F32), pltpu.VMEM((n, hps * dv), F32),
            pltpu.VMEM((n, hps * dk), F32), pltpu.VMEM((n, hps * dk), F32)]


def _gla_ctx(p, plr, w2p, gla_b, gla_norm, sf_prev, sb_prev, l, dims):
    n, nseq, gh, dk, dv, depth = dims["s"], dims["b"], dims["gh"], dims["dk"], dims["dv"], dims["depth"]
    hps = _gla_heads_per_step(dims)
    in_specs = _gla_common_specs(l, dims, n, 0, hps)
    args = [p, p, p, p, plr, w2p, gla_b, gla_norm]
    st_shape = jax.ShapeDtypeStruct((nseq, depth, gh, dk, dv), F32)
    first = sf_prev is None
    aliases = {}
    if first:
        st_spec = pl.BlockSpec((None, depth, hps, dk, dv), lambda b, h: (b, 0, h, 0, 0))
    else:
        in_specs += [_any_spec(), _any_spec()]
        args += [sf_prev, sb_prev]
        aliases = {8: 1, 9: 2}
        st_spec = pl.BlockSpec((None, None, hps, dk, dv), lambda b, h: (b, l, h, 0, 0))
    return pl.pallas_call(
        functools.partial(_gla_kernel, n=n, dk=dk, dv=dv, hps=hps, latent=False, fill_slots=first),
        out_shape=[jax.ShapeDtypeStruct((nseq * n, gh * dv), BF16), st_shape, st_shape],
        grid=(nseq, gh // hps), in_specs=in_specs,
        out_specs=[pl.BlockSpec((n, hps * dv), lambda b, h: (b, h)), st_spec, st_spec],
        scratch_shapes=_gla_scratch(n, dk, dv, hps), input_output_aliases=aliases,
        compiler_params=_params(("arbitrary", "arbitrary"), 40), name="gla_ctx",
    )(*args)


def _gla_lat(p, plr, w2p, gla_b, gla_norm, state_f, state_b, l, dims):
    n, nseq, gh, dk, dv = dims["sd"], dims["bd"], dims["gh"], dims["dk"], dims["dv"]
    hps = _gla_heads_per_step(dims)
    rb0 = dims["row_lat"] // n
    in_specs = _gla_common_specs(l, dims, n, rb0, hps)
    s_spec = pl.BlockSpec((None, None, hps, dk, dv), lambda b, h: (b, l, h, 0, 0))
    in_specs += [s_spec, s_spec]
    return pl.pallas_call(
        functools.partial(_gla_kernel, n=n, dk=dk, dv=dv, hps=hps, latent=True, fill_slots=False),
        out_shape=jax.ShapeDtypeStruct((nseq * n, gh * dv), BF16),
        grid=(nseq, gh // hps), in_specs=in_specs,
        out_specs=pl.BlockSpec((n, hps * dv), lambda b, h: (b, h)),
        scratch_shapes=_gla_scratch(n, dk, dv, hps),
        compiler_params=_params(("arbitrary", "arbitrary"), 56), name="gla_lat",
    )(p, p, p, p, plr, w2p, gla_b, gla_norm, state_f, state_b)


SEQ_TILE_ELEMS = 1 << 20


def _seq_col_tile(n, d):
    return _pick(d, tuple(c for c in (4096, 2048, 1024, 512, 256, 128) if n * c <= SEQ_TILE_ELEMS))


def _router_kernel(*refs, n, n_exp, cap):
    lg_ref, h_ref = refs[:2]
    xs_ref, g_ref, pg_ref, colb_ref, afft_ref = refs[-5:]
    nt = n // LANES

    @pl.when(pl.program_id(1) == 0)
    def _route():
        lg = lg_ref[...]
        lane = lax.broadcasted_iota(jnp.int32, lg.shape, 1)
        lg = jnp.where(lane < n_exp, lg, -jnp.inf)
        ex = jnp.exp(lg - jnp.max(lg, axis=-1, keepdims=True))
        aff = ex / jnp.sum(ex, axis=-1, keepdims=True)
        afft = aff.T
        for e in range(n_exp):
            afft_ref[e] = afft[e:e + 1, :]
            colb_ref[e] = jnp.broadcast_to(aff[:, e:e + 1], (n, LANES))
        si = lax.broadcasted_iota(jnp.int32, (LANES, LANES), 0)
        ti = lax.broadcasted_iota(jnp.int32, (LANES, LANES), 1)
        earlier = si < ti
        slot = lax.broadcasted_iota(jnp.int32, (cap, n), 0).astype(F32)

        def per_expert(e, carry):
            parts = []
            for tt in range(nt):
                row = afft_ref[e, :, tt * LANES:(tt + 1) * LANES]
                acc = jnp.zeros((LANES, LANES), F32)
                for c in range(nt):
                    col = colb_ref[e, c * LANES:(c + 1) * LANES, :]
                    if c < tt:
                        ahead = jnp.where(col >= row, 1.0, 0.0)
                    elif c > tt:
                        ahead = jnp.where(col > row, 1.0, 0.0)
                    else:
                        ahead = jnp.where(earlier, jnp.where(col >= row, 1.0, 0.0), jnp.where(col > row, 1.0, 0.0))
                    acc = acc + ahead
                parts.append(jnp.sum(acc, axis=0, keepdims=True))
            rank = jnp.concatenate(parts, axis=1)
            sel = rank == slot
            pg_ref[pl.ds(pl.multiple_of(e * cap, cap), cap), :] = jnp.where(sel, 1.0, 0.0).astype(BF16)
            g_ref[e] = jnp.sum(jnp.where(sel, afft_ref[e], 0.0), axis=1, keepdims=True)
            return carry

        lax.fori_loop(0, n_exp, per_expert, 0, unroll=4 if nt <= 2 else 1)

    xs = jnp.dot(pg_ref[...], h_ref[...], preferred_element_type=F32)
    for e in range(n_exp):
        xs_ref[e] = xs[e * cap:(e + 1) * cap, :].astype(xs_ref.dtype)


def _router(logits, h2, n, nseq, row0, n_exp):
    t, d = h2.shape
    cap = EC_FACTOR * n // n_exp
    td = _seq_col_tile(n, d)
    rb0 = row0 // n
    return pl.pallas_call(
        functools.partial(_router_kernel, n=n, n_exp=n_exp, cap=cap),
        out_shape=[jax.ShapeDtypeStruct((n_exp, nseq * cap, d), BF16),
                   jax.ShapeDtypeStruct((n_exp, nseq * cap, 1), F32),
                   jax.ShapeDtypeStruct((nseq, n_exp * cap, n), BF16)],
        grid=(nseq, d // td),
        in_specs=[pl.BlockSpec((n, LANES), lambda b, j: (rb0 + b, 0)),
                  pl.BlockSpec((n, td), lambda b, j: (rb0 + b, j))],
        out_specs=[pl.BlockSpec((n_exp, cap, td), lambda b, j: (0, b, j)),
                   pl.BlockSpec((n_exp, cap, 1), lambda b, j: (0, b, 0)),
                   pl.BlockSpec((None, n_exp * cap, n), lambda b, j: (b, 0, 0))],
        scratch_shapes=[pltpu.VMEM((n_exp, n, LANES), F32), pltpu.VMEM((n_exp, 1, n), F32)],
        compiler_params=_params(("arbitrary", "arbitrary"), 48), name="router",
    )(logits, h2)


def _scatter_kernel(pg_ref, ye_ref, x_ref, ga_ref, o_ref, *, n_exp):
    ye = jnp.concatenate([ye_ref[e] for e in range(n_exp)], axis=0)
    y = lax.dot_general(pg_ref[...], ye, TN_DIMS, preferred_element_type=F32)
    o_ref[...] = x_ref[...] + ga_ref[...] * y


def _scatter(pg, ye, x1_part, modrows, ga_off, n, row0, slot0):
    rows, d = x1_part.shape
    nseq = rows // n
    n_exp = ye.shape[0]
    cap = pg.shape[1] // n_exp
    td = _seq_col_tile(n, d)
    sb0 = slot0 // cap
    mb0 = row0 // MOD_BLK
    mstep = n // MOD_BLK
    gb = ga_off // td
    return pl.pallas_call(
        functools.partial(_scatter_kernel, n_exp=n_exp),
        out_shape=jax.ShapeDtypeStruct((rows, d), F32), grid=(nseq, d // td),
        in_specs=[pl.BlockSpec((None, n_exp * cap, n), lambda b, j: (b, 0, 0)),
                  pl.BlockSpec((n_exp, cap, td), lambda b, j: (0, sb0 + b, j)),
                  pl.BlockSpec((n, td), lambda b, j: (b, j)),
                  pl.BlockSpec((None, 1, td), lambda b, j: (mb0 + b * mstep, 0, gb + j))],
        out_specs=pl.BlockSpec((n, td), lambda b, j: (b, j)),
        compiler_params=_params(("arbitrary", "arbitrary"), 48), name="scatter",
    )(pg, ye, x1_part, modrows)


def _rope_tables(n, hd):
    qtr = hd // 4
    rows = n // GRID_W
    rpos = jnp.repeat(jnp.arange(rows, dtype=jnp.int32), GRID_W)
    cpos = jnp.arange(rows * GRID_W, dtype=jnp.int32) % GRID_W
    inv = ROPE_THETA ** (-jnp.arange(qtr, dtype=F32) / qtr)
    ar = rpos.astype(F32)[:, None] * inv[None, :]
    ac = cpos.astype(F32)[:, None] * inv[None, :]
    cos = jnp.concatenate([jnp.cos(ar), jnp.cos(ar), jnp.cos(ac), jnp.cos(ac)], axis=-1)
    sin = jnp.concatenate([-jnp.sin(ar), jnp.sin(ar), -jnp.sin(ac), jnp.sin(ac)], axis=-1)
    return cos, sin


def kernel(x_prompt, x_sample, cache_k, cache_v, state_gla_fwd, state_gla_bwd, c, c_ctx, w_mod, b_mod, norm1, norm2, w_in, conv_w, q_norm, k_norm, gla_w2, gla_b, gla_norm, w_conv_out, w_att_out, w_gla_out, w_o, w_router, w_gate, w_up, w_down):
    b, s, d = x_prompt.shape
    bd, sd, _ = x_sample.shape
    depth = w_mod.shape[0]
    cw = conv_w.shape[-1]
    hd = q_norm.shape[-1]
    nkv = cache_k.shape[3]
    aw = w_att_out.shape[1]
    groups = aw // hd // nkv
    gh, dk, dv = state_gla_fwd.shape[2:]
    rank = gla_w2.shape[2]
    n_exp = w_router.shape[-1]
    t = b * s + bd * sd
    row_lat = b * s

    n_conv = 3 * cw
    o_aq = 0
    o_ak = o_aq + aw
    o_av = o_ak + nkv * hd
    o_gq = o_av + nkv * hd
    o_gk = o_gq + gh * dk
    o_gv = o_gk + gh * dk
    o_gr = o_gv + gh * dv
    n_mix = o_gr + gh * dv
    o_lr = n_conv + n_mix
    o_m = o_lr + 2 * rank
    assert s % MOD_BLK == 0 and sd % MOD_BLK == 0 and sd % GRID_W == 0 and s % LANES == 0
    assert row_lat % sd == 0 and 2 * rank <= LANES and o_lr + LANES <= w_in.shape[-1]
    dims = dict(t=t, s=s, sd=sd, b=b, bd=bd, nkv=nkv, groups=groups, hd=hd, gh=gh, dk=dk, dv=dv, depth=depth,
                row_lat=row_lat, o_aq=o_aq, o_ak=o_ak, o_av=o_av, o_gq=o_gq, o_gk=o_gk, o_gv=o_gv, o_gr=o_gr,
                o_lr=o_lr)

    cap_c = EC_FACTOR * s // n_exp
    cap_l = EC_FACTOR * sd // n_exp
    mx = b * cap_c + bd * cap_l
    assert (b * cap_c) % cap_l == 0

    cond8 = jnp.zeros((8, d), F32).at[0].set(c_ctx).at[1:1 + bd].set(c)
    mod_all = _modulation(cond8, w_mod, b_mod)
    blk_row = np.concatenate([np.zeros(b * s // MOD_BLK, np.int32),
                              np.repeat(np.arange(1, bd + 1, dtype=np.int32), sd // MOD_BLK)])

    x_ctx = x_prompt.reshape(b * s, d)
    x_lat = x_sample.reshape(bd * sd, d)
    cos, sin = _rope_tables(sd, hd)
    w2p = jnp.zeros((depth, 2, LANES, gh * dk), F32)
    w2p = w2p.at[:, 0, :rank].set(gla_w2[:, 0]).at[:, 1, rank:2 * rank].set(gla_w2[:, 1])
    wr_pad = jnp.zeros((depth, d, LANES), F32).at[:, :, :n_exp].set(w_router)
    q_norm3 = q_norm.reshape(depth, 1, hd)
    k_norm3 = k_norm.reshape(depth, 1, hd)
    gla_norm3 = gla_norm.reshape(depth, 1, dv)
    cache_k4 = cache_k.reshape(bd, depth, cache_k.shape[2], nkv * hd)
    cache_v4 = cache_v.reshape(bd, depth, cache_v.shape[2], nkv * hd)
    w_in_t = jnp.swapaxes(w_in, 1, 2)
    w_conv_out_b, w_att_out_b, w_gla_out_b = (w.astype(BF16) for w in (w_conv_out, w_att_out, w_gla_out))

    new_k, new_v = [], []
    sf = sb = None
    for l in range(depth):
        modrows = mod_all[l][blk_row][:, None, :]
        (h,) = _normmod(x_ctx, x_lat, modrows, norm1[l], 0, d)
        p_conv = _proj(h, w_in_t, l, 0, n_conv, BF16, _epi_identity, "conv_proj")
        p = _proj(h, w_in_t, l, n_conv, n_mix, F32, _epi_identity, "in_proj")
        plr = _proj(h, w_in_t, l, o_lr, LANES, F32, _epi_identity, "lr_proj", tn=LANES)

        y_conv = _conv(p_conv, conv_w, l, t, cw, s, sd, row_lat)

        y_att_c, k_new = _attn_ctx(p, q_norm3, k_norm3, l, dims)
        y_att_l = _attn_lat(p, q_norm3, k_norm3, cos, sin, cache_k4, cache_v4, l, dims)
        new_k.append(k_new.reshape(b, s, nkv, hd))
        new_v.append(p[:row_lat, o_av:o_av + nkv * hd].reshape(b, s, nkv, hd))

        y_gla_c, sf, sb = _gla_ctx(p, plr, w2p, gla_b, gla_norm3, sf, sb, l, dims)
        y_gla_l = _gla_lat(p, plr, w2p, gla_b, gla_norm3, state_gla_fwd, state_gla_bwd, l, dims)

        gates = _proj(h, w_in_t, l, o_m, 3 * d, BF16, _epi_sigmoid, "gate_proj")
        out_w = (w_conv_out_b, w_att_out_b, w_gla_out_b)
        merged_c = _merge(y_conv, y_att_c, y_gla_c, gates, *out_w, l, d, 0)
        merged_l = _merge(y_conv, y_att_l, y_gla_l, gates, *out_w, l, d, row_lat)
        x1_ctx = _out_proj(merged_c, w_o, l, x_ctx, 0, modrows, 2 * d)
        x1_lat = _out_proj(merged_l, w_o, l, x_lat, row_lat, modrows, 2 * d)

        h2, logits = _normmod(x1_ctx, x1_lat, modrows, norm2[l], 3 * d, 4 * d, w_router_pad=wr_pad[l])
        xs_c, g_c, pg_c = _router(logits, h2, s, b, 0, n_exp)
        xs_l, g_l, pg_l = _router(logits, h2, sd, bd, row_lat, n_exp)
        hid = _ffn_up((xs_c, xs_l), w_gate, w_up, l)
        ye = _ffn_down(hid, w_down, (g_c, g_l), l)
        x_ctx = _scatter(pg_c, ye, x1_ctx, modrows, 5 * d, s, 0, 0)
        x_lat = _scatter(pg_l, ye, x1_lat, modrows, 5 * d, sd, row_lat, b * cap_c)

    y_prompt = x_ctx.reshape(b, s, d)
    y_sample = x_lat.reshape(bd, sd, d)
    return (y_prompt, y_sample, jnp.stack(new_k, axis=1), jnp.stack(new_v, axis=1), sf, sb)
```

```python
import functools

import jax
import jax.numpy as jnp
import numpy as np
from jax import lax
from jax.experimental import pallas as pl
from jax.experimental.pallas import tpu as pltpu

F32 = jnp.float32
BF16 = jnp.bfloat16

GRID_W = 64
GLA_CHUNK = 64
GLA_TAU = 16.0
EC_FACTOR = 2
ROPE_THETA = 10000.0
EPS = 1e-6
LOG2_E = 1.4426950408889634

LANES = 128
SUBLANES = 8
MOD_BLK = 256
GLA_GROUP = 256
NT_DIMS = (((1,), (1,)), ((), ()))
TN_DIMS = (((0,), (0,)), ((), ()))


def _params(sem, vmem_mb):
    return pltpu.CompilerParams(dimension_semantics=sem, vmem_limit_bytes=vmem_mb << 20)


def _any_spec():
    return pl.BlockSpec(memory_space=pl.ANY)


def _sigmoid(x):
    return 0.5 * jnp.tanh(0.5 * x) + 0.5


def _silu(x):
    return x * _sigmoid(x)


def _dot_split(a, b):
    a1 = a.astype(BF16)
    a2 = (a - a1.astype(F32)).astype(BF16)
    b1 = b.astype(BF16)
    b2 = (b - b1.astype(F32)).astype(BF16)
    return (jnp.dot(a1, b1, preferred_element_type=F32) + jnp.dot(a1, b2, preferred_element_type=F32)
            + jnp.dot(a2, b1, preferred_element_type=F32))


def _pick(total, candidates):
    for c in candidates:
        if total % c == 0:
            return c
    raise ValueError(f"no tile in {candidates} divides {total}")


def _mod_kernel(c_ref, w_ref, b_ref, o_ref):
    s = _silu(c_ref[...]).astype(BF16)
    o_ref[...] = jnp.dot(s, w_ref[...].astype(BF16), preferred_element_type=F32) + b_ref[...]


def _modulation(cond8, w_mod, b_mod):
    depth, d, n6 = w_mod.shape
    tn = _pick(n6, (512, 256, 128))
    return pl.pallas_call(
        _mod_kernel,
        out_shape=jax.ShapeDtypeStruct((depth, 8, n6), F32),
        grid=(depth, n6 // tn),
        in_specs=[pl.BlockSpec((8, d), lambda l, j: (0, 0)),
                  pl.BlockSpec((None, d, tn), lambda l, j: (l, 0, j)),
                  pl.BlockSpec((None, 1, tn), lambda l, j: (l, 0, j))],
        out_specs=pl.BlockSpec((None, 8, tn), lambda l, j: (l, 0, j)),
        compiler_params=_params(("arbitrary", "arbitrary"), 40),
        name="modulation",
    )(cond8, w_mod, b_mod.reshape(depth, 1, n6))


def _normmod_kernel(xc_ref, xl_ref, m_ref, g_ref, *rest, d, sh_off, sc_off, router, ctx_blocks):
    def emit(x_ref):
        x = x_ref[...]
        y = x * lax.rsqrt(jnp.mean(x * x, axis=-1, keepdims=True) + EPS) * g_ref[...]
        h = y * (1.0 + m_ref[:, sc_off:sc_off + d]) + m_ref[:, sh_off:sh_off + d]
        if router:
            wr_ref, o_ref, lg_ref = rest
            lg_ref[...] = _dot_split(h, wr_ref[...])
        else:
            (o_ref,) = rest
        o_ref[...] = h.astype(BF16)

    pl.when(pl.program_id(0) < ctx_blocks)(lambda: emit(xc_ref))
    pl.when(pl.program_id(0) >= ctx_blocks)(lambda: emit(xl_ref))


def _normmod(x_ctx, x_lat, modrows, gain, sh_off, sc_off, w_router_pad=None):
    (rc, d), rl = x_ctx.shape, x_lat.shape[0]
    router = w_router_pad is not None
    nc, nl = rc // MOD_BLK, rl // MOD_BLK
    in_specs = [pl.BlockSpec((MOD_BLK, d), lambda i: (jnp.minimum(i, nc - 1), 0)),
                pl.BlockSpec((MOD_BLK, d), lambda i: (jnp.maximum(i - nc, 0), 0)),
                pl.BlockSpec((None, 1, 6 * d), lambda i: (i, 0, 0)),
                pl.BlockSpec((1, d), lambda i: (0, 0))]
    args = [x_ctx, x_lat, modrows, gain.reshape(1, d)]
    out_shape = [jax.ShapeDtypeStruct((rc + rl, d), BF16)]
    out_specs = [pl.BlockSpec((MOD_BLK, d), lambda i: (i, 0))]
    if router:
        in_specs.append(pl.BlockSpec((d, LANES), lambda i: (0, 0)))
        args.append(w_router_pad)
        out_shape.append(jax.ShapeDtypeStruct((rc + rl, LANES), F32))
        out_specs.append(pl.BlockSpec((MOD_BLK, LANES), lambda i: (i, 0)))
    return pl.pallas_call(
        functools.partial(_normmod_kernel, d=d, sh_off=sh_off, sc_off=sc_off, router=router, ctx_blocks=nc),
        out_shape=out_shape, grid=(nc + nl,), in_specs=in_specs, out_specs=out_specs,
        compiler_params=_params(("arbitrary",), 48),
        name="normmod_router" if router else "normmod",
    )(*args)


def _mm_kernel(*refs, n_a, n_w, n_sides, w_rows, row_parts, epilogue):
    a_refs = refs[:n_a]
    w_refs = refs[n_a:n_a + n_w]
    side_refs = refs[n_a + n_w:n_a + n_w + n_sides]
    o_ref = refs[n_a + n_w + n_sides]
    wbs = [(w_ref[0] if w_rows else w_ref[...]).astype(BF16) for w_ref in w_refs]

    def dot(a, wb):
        if w_rows:
            return lax.dot_general(a, wb, NT_DIMS, preferred_element_type=F32)
        return jnp.dot(a, wb, preferred_element_type=F32)

    sides = [s[...] for s in side_refs]
    if row_parts:
        r0 = 0
        for a_ref in a_refs:
            r1 = r0 + a_ref.shape[0]
            o_ref[r0:r1, :] = epilogue([dot(a_ref[...], wb) for wb in wbs], sides).astype(o_ref.dtype)
            r0 = r1
    else:
        accs = [dot(a_refs[q if n_a > 1 else 0][...], wb) for q, wb in enumerate(wbs)]
        o_ref[...] = epilogue(accs, sides).astype(o_ref.dtype)


def _fused_mm(grid, a_args, w_args, side_args, out_shape, out_spec, epilogue, vmem_mb, name, w_rows=False,
              row_parts=False):
    assert row_parts or len(a_args) in (1, len(w_args))
    arrays = [a for a, _ in a_args] + [w for w, _ in w_args] + [s for s, _ in side_args]
    specs = [s for _, s in a_args] + [s for _, s in w_args] + [s for _, s in side_args]
    body = functools.partial(_mm_kernel, n_a=len(a_args), n_w=len(w_args), n_sides=len(side_args),
                             w_rows=w_rows, row_parts=row_parts, epilogue=epilogue)
    return pl.pallas_call(
        body, out_shape=out_shape, grid=grid, in_specs=specs, out_specs=out_spec,
        compiler_params=_params(("arbitrary",) * len(grid), vmem_mb), name=name,
    )(*arrays)


def _epi_identity(accs, sides):
    return accs[0]


def _epi_sigmoid(accs, sides):
    return _sigmoid(accs[0])


def _epi_residual(accs, sides):
    x, ga = sides
    return x + ga * accs[0]


def _epi_swiglu(accs, sides):
    return _silu(accs[0]) * accs[1]


def _epi_merge(accs, sides):
    return sides[0] * accs[0] + sides[1] * accs[1] + sides[2] * accs[2]


def _epi_rowscale(accs, sides):
    return accs[0] * jnp.concatenate(sides, axis=0)


def _proj(h, w_t, l, row0, n_cols, out_dtype, epilogue, name, tn=512):
    t, k = h.shape
    tm = _pick(t, (1536, 1024, 512, 256))
    assert row0 % SUBLANES == 0 and n_cols % tn == 0
    elem = (pl.Element(1), pl.Element(tn), pl.Element(k))
    return _fused_mm(
        (t // tm, n_cols // tn),
        [(h, pl.BlockSpec((tm, k), lambda i, j: (i, 0)))],
        [(w_t, pl.BlockSpec(elem, lambda i, j: (l, pl.multiple_of(row0 + j * tn, SUBLANES), 0)))],
        [], jax.ShapeDtypeStruct((t, n_cols), out_dtype),
        pl.BlockSpec((tm, tn), lambda i, j: (i, j)), epilogue, 58, name, w_rows=True)


def _merge(y_conv, y_att, y_gla, gates, w_conv_out, w_att_out, w_gla_out, l, d, row0):
    rows = y_att.shape[0]
    tm = _pick(rows, (1024, 512, 256))
    tn = _pick(d, (512, 256, 128))
    assert row0 % tm == 0
    ib0 = row0 // tm
    nb = d // tn
    ws = (w_conv_out, w_att_out, w_gla_out)
    a_args = [(y_conv, pl.BlockSpec((tm, y_conv.shape[1]), lambda i, j: (ib0 + i, 0))),
              (y_att, pl.BlockSpec((tm, y_att.shape[1]), lambda i, j: (i, 0))),
              (y_gla, pl.BlockSpec((tm, y_gla.shape[1]), lambda i, j: (i, 0)))]
    w_args = [(w, pl.BlockSpec((None, w.shape[1], tn), lambda i, j: (l, 0, j))) for w in ws]
    side_args = [(gates, pl.BlockSpec((tm, tn), functools.partial(lambda i, j, q: (ib0 + i, q * nb + j), q=q)))
                 for q in range(3)]
    return _fused_mm((rows // tm, nb), a_args, w_args, side_args,
                     jax.ShapeDtypeStruct((rows, d), BF16), pl.BlockSpec((tm, tn), lambda i, j: (i, j)),
                     _epi_merge, 58, "merge")


def _out_proj(merged, w_o, l, x_part, row0, modrows, ga_off):
    rows, d = x_part.shape
    tm = _pick(rows, (1024, 512, 256))
    tn = _pick(d, (512, 256, 128))
    mb0 = row0 // MOD_BLK
    rb = tm // MOD_BLK
    gb = ga_off // tn
    return _fused_mm(
        (rows // tm, d // tn),
        [(merged, pl.BlockSpec((tm, d), lambda i, j: (i, 0)))],
        [(w_o, pl.BlockSpec((None, d, tn), lambda i, j: (l, 0, j)))],
        [(x_part, pl.BlockSpec((tm, tn), lambda i, j: (i, j))),
         (modrows, pl.BlockSpec((None, 1, tn), lambda i, j: (mb0 + i * rb, 0, gb + j)))],
        jax.ShapeDtypeStruct((rows, d), F32), pl.BlockSpec((tm, tn), lambda i, j: (i, j)),
        _epi_residual, 52, "out_proj")


def _ffn_up(xs_parts, w_gate, w_up, l):
    e, _, d = xs_parts[0].shape
    mx = sum(x.shape[1] for x in xs_parts)
    f = w_gate.shape[-1]
    tf = _pick(f, (256, 128))
    wspec = pl.BlockSpec((None, None, d, tf), lambda ee, j: (l, ee, 0, j))
    return _fused_mm(
        (e, f // tf),
        [(x, pl.BlockSpec((None, x.shape[1], d), lambda ee, j: (ee, 0, 0))) for x in xs_parts],
        [(w_gate, wspec), (w_up, wspec)], [],
        jax.ShapeDtypeStruct((e, mx, f), BF16), pl.BlockSpec((None, mx, tf), lambda ee, j: (ee, 0, j)),
        _epi_swiglu, 58, "ffn_up", row_parts=True)


def _ffn_down(hid, w_down, gate_parts, l):
    e, mx, f = hid.shape
    d = w_down.shape[-1]
    tn = _pick(d, (1024, 512, 256, 128))
    return _fused_mm(
        (e, d // tn),
        [(hid, pl.BlockSpec((None, mx, f), lambda ee, j: (ee, 0, 0)))],
        [(w_down, pl.BlockSpec((None, None, f, tn), lambda ee, j: (l, ee, 0, j)))],
        [(g, pl.BlockSpec((None, g.shape[1], 1), lambda ee, j: (ee, 0, 0))) for g in gate_parts],
        jax.ShapeDtypeStruct((e, mx, d), BF16), pl.BlockSpec((None, mx, tn), lambda ee, j: (ee, 0, j)),
        _epi_rowscale, 56, "ffn_down")


def _conv_kernel(cb_ref, cc_ref, ch_ref, w_ref, o_ref, *, ctx_blocks, s, sd):
    u = cc_ref[...].astype(F32) * ch_ref[...].astype(F32)
    n = u.shape[0]
    row = lax.broadcasted_iota(jnp.int32, u.shape, 0)
    last = jnp.where(pl.program_id(0) < ctx_blocks, s - 1, sd - 1)
    pos = row & last
    prev = jnp.where(pos == 0, 0.0, pltpu.roll(u, 1, 0))
    nxt = jnp.where(pos == last, 0.0, pltpu.roll(u, n - 1, 0))
    z = w_ref[0:1, :] * prev + w_ref[1:2, :] * u + w_ref[2:3, :] * nxt
    o_ref[...] = (cb_ref[...].astype(F32) * z).astype(o_ref.dtype)


def _conv(p, conv_w, l, t, cw, s, sd, row_lat):
    assert s & (s - 1) == 0 and sd & (sd - 1) == 0 and sd % s == 0 and row_lat % sd == 0
    tc = _pick(cw, (512, 256, 128))
    nb = cw // tc
    return pl.pallas_call(
        functools.partial(_conv_kernel, ctx_blocks=row_lat // sd, s=s, sd=sd),
        out_shape=jax.ShapeDtypeStruct((t, cw), BF16), grid=(t // sd, nb),
        in_specs=[pl.BlockSpec((sd, tc), lambda b, j: (b, j)),
                  pl.BlockSpec((sd, tc), lambda b, j: (b, nb + j)),
                  pl.BlockSpec((sd, tc), lambda b, j: (b, 2 * nb + j)),
                  pl.BlockSpec((None, 3, tc), lambda b, j: (l, 0, j))],
        out_specs=pl.BlockSpec((sd, tc), lambda b, j: (b, j)),
        compiler_params=_params(("arbitrary", "arbitrary"), 40), name="conv",
    )(p, p, p, conv_w)


def _head_norm(x, gain):
    return x * lax.rsqrt(jnp.mean(x * x, axis=-1, keepdims=True) + EPS) * gain


def _rope(x, cos, sin_signed):
    hd = x.shape[-1]
    lane = lax.broadcasted_iota(jnp.int32, x.shape, 1)
    q = hd // 4
    swapped = jnp.where((lane & q) == 0, pltpu.roll(x, hd - q, 1), pltpu.roll(x, q, 1))
    return x * cos + swapped * sin_signed


def _attn_kernel(*refs, nkv, groups, hd, latent):
    if latent:
        (q_ref, k_ref, v_ref, qg_ref, kg_ref, cosq_ref, sinq_ref, cosk_ref, sink_ref,
         ck_ref, cv_ref, y_ref, kb_ref) = refs

        @pl.when(pl.program_id(1) == 0)
        def _keys():
            for kv in range(nkv):
                ks = slice(kv * hd, (kv + 1) * hd)
                kn = _rope(_head_norm(k_ref[:, ks], kg_ref[...]), cosk_ref[...], sink_ref[...])
                kb_ref[:, ks] = kn.astype(BF16)
    else:
        q_ref, k_ref, v_ref, qg_ref, kg_ref, y_ref, kout_ref = refs
    c = (hd ** -0.5) * LOG2_E
    for kv in range(nkv):
        ks = slice(kv * hd, (kv + 1) * hd)
        if latent:
            kb = kb_ref[:, ks]
            ckb = ck_ref[:, ks].astype(BF16)
            cvb = cv_ref[:, ks].astype(BF16)
        else:
            kn = _head_norm(k_ref[:, ks], kg_ref[...])
            kout_ref[:, ks] = kn
            kb = kn.astype(BF16)
        vb = v_ref[:, ks].astype(BF16)
        for g in range(groups):
            qs = slice((kv * groups + g) * hd, (kv * groups + g + 1) * hd)
            qn = _head_norm(q_ref[:, qs], qg_ref[...])
            if latent:
                qn = _rope(qn, cosq_ref[...], sinq_ref[...])
            qb = qn.astype(BF16)
            s = lax.dot_general(qb, kb, NT_DIMS, preferred_element_type=F32)
            m = jnp.max(s, axis=-1, keepdims=True)
            if latent:
                s2 = lax.dot_general(qb, ckb, NT_DIMS, preferred_element_type=F32)
                m = jnp.maximum(m, jnp.max(s2, axis=-1, keepdims=True))
                p2 = jnp.exp2((s2 - m) * c)
            p = jnp.exp2((s - m) * c)
            den = jnp.sum(p, axis=-1, keepdims=True)
            o = jnp.dot(p.astype(BF16), vb, preferred_element_type=F32)
            if latent:
                den = den + jnp.sum(p2, axis=-1, keepdims=True)
                o = o + jnp.dot(p2.astype(BF16), cvb, preferred_element_type=F32)
            y_ref[:, qs] = (o / den).astype(y_ref.dtype)


def _attn_ctx(p, q_norm, k_norm, l, dims):
    t, n, nseq, nkv, groups, hd = dims["t"], dims["s"], dims["b"], dims["nkv"], dims["groups"], dims["hd"]
    aw, kw = nkv * groups * hd, nkv * hd
    assert dims["o_aq"] % aw == 0 and dims["o_ak"] % kw == 0 and dims["o_av"] % kw == 0
    qb0, kb0, vb0 = dims["o_aq"] // aw, dims["o_ak"] // kw, dims["o_av"] // kw
    return pl.pallas_call(
        functools.partial(_attn_kernel, nkv=nkv, groups=groups, hd=hd, latent=False),
        out_shape=[jax.ShapeDtypeStruct((nseq * n, aw), BF16), jax.ShapeDtypeStruct((nseq * n, kw), F32)],
        grid=(nseq,),
        in_specs=[pl.BlockSpec((n, aw), lambda b: (b, qb0)),
                  pl.BlockSpec((n, kw), lambda b: (b, kb0)),
                  pl.BlockSpec((n, kw), lambda b: (b, vb0)),
                  pl.BlockSpec((None, 1, hd), lambda b: (l, 0, 0)),
                  pl.BlockSpec((None, 1, hd), lambda b: (l, 0, 0))],
        out_specs=[pl.BlockSpec((n, aw), lambda b: (b, 0)),
                   pl.BlockSpec((n, kw), lambda b: (b, 0))],
        compiler_params=_params(("arbitrary",), 40), name="attn_ctx",
    )(p, p, p, q_norm, k_norm)


def _attn_lat(p, q_norm, k_norm, cos, sin, cache_k, cache_v, l, dims):
    t, n, nseq, nkv, groups, hd = dims["t"], dims["sd"], dims["bd"], dims["nkv"], dims["groups"], dims["hd"]
    past = cache_k.shape[2]
    aw, kw = nkv * groups * hd, nkv * hd
    tq = _pick(n, (256, 128))
    nq = n // tq
    qb0, kb0, vb0 = dims["o_aq"] // aw, dims["o_ak"] // kw, dims["o_av"] // kw
    rq0 = dims["row_lat"] // tq
    rk0 = dims["row_lat"] // n
    return pl.pallas_call(
        functools.partial(_attn_kernel, nkv=nkv, groups=groups, hd=hd, latent=True),
        out_shape=jax.ShapeDtypeStruct((nseq * n, aw), BF16),
        grid=(nseq, nq),
        in_specs=[pl.BlockSpec((tq, aw), lambda b, i: (rq0 + b * nq + i, qb0)),
                  pl.BlockSpec((n, kw), lambda b, i: (rk0 + b, kb0)),
                  pl.BlockSpec((n, kw), lambda b, i: (rk0 + b, vb0)),
                  pl.BlockSpec((None, 1, hd), lambda b, i: (l, 0, 0)),
                  pl.BlockSpec((None, 1, hd), lambda b, i: (l, 0, 0)),
                  pl.BlockSpec((tq, hd), lambda b, i: (i, 0)),
                  pl.BlockSpec((tq, hd), lambda b, i: (i, 0)),
                  pl.BlockSpec((n, hd), lambda b, i: (0, 0)),
                  pl.BlockSpec((n, hd), lambda b, i: (0, 0)),
                  pl.BlockSpec((None, None, past, kw), lambda b, i: (b, l, 0, 0)),
                  pl.BlockSpec((None, None, past, kw), lambda b, i: (b, l, 0, 0))],
        out_specs=pl.BlockSpec((tq, aw), lambda b, i: (b * nq + i, 0)),
        scratch_shapes=[pltpu.VMEM((n, kw), BF16)],
        compiler_params=_params(("arbitrary",) * 2, 48), name="attn_lat",
    )(p, p, p, q_norm, k_norm, cos, sin, cos, sin, cache_k, cache_v)


def _log_sigmoid(z):
    return jnp.minimum(z, 0.0) - jnp.log1p(jnp.exp(-jnp.abs(z)))


def _dot01(m01, x):
    x1 = x.astype(BF16)
    r1 = x - x1.astype(F32)
    x2 = r1.astype(BF16)
    x3 = (r1 - x2.astype(F32)).astype(BF16)
    return (jnp.dot(m01, x1, preferred_element_type=F32) + jnp.dot(m01, x2, preferred_element_type=F32)
            + jnp.dot(m01, x3, preferred_element_type=F32))


def _gla_kernel(*refs, n, dk, dv, hps, latent, fill_slots):
    q_ref, k_ref, v_ref, r_ref, lr_ref, w2_ref, b_ref, gn_ref = refs[:8]
    stf_ref, stb_ref, of_ref, ob_ref, gf_ref, gb_ref = refs[-6:]
    if latent:
        sf0_ref, sb0_ref, y_ref = refs[8:11]
        sf_ref = sb_ref = None
    else:
        y_ref, sf_ref, sb_ref = refs[-9:-6]
        sf0_ref = sb0_ref = None
    L = GLA_CHUNK
    G = min(GLA_GROUP, n)
    cpg = G // L
    ng = n // G
    scale = dk ** -0.5

    lr = lr_ref[...]
    for d, g_ref in ((0, gf_ref), (1, gb_ref)):
        z = _dot_split(lr, w2_ref[d]) + b_ref[d:d + 1, :]
        g_ref[...] = _log_sigmoid(z) / GLA_TAU

    ri = lax.broadcasted_iota(jnp.int32, (G, G), 0)
    ci = lax.broadcasted_iota(jnp.int32, (G, G), 1)
    same_chunk = (ri // L) == (ci // L)

    def run(forward, hh, g_ref, s0_ref, s_out_ref, st_ref, o_ref):
        kc_ = slice(hh * dk, (hh + 1) * dk)
        vc_ = slice(hh * dv, (hh + 1) * dv)
        tri = same_chunk & ((ri >= ci) if forward else (ri <= ci))
        tri01 = jnp.where(tri, 1.0, 0.0).astype(BF16)
        if forward:
            pair = (ri // L == ci // L + 1) & ((ci // L) % 2 == 0)
        else:
            pair = (ri // L == ci // L - 1) & ((ri // L) % 2 == 0)
        st_ref[hh] = jnp.zeros((dk, dv), F32) if s0_ref is None else s0_ref[hh]
        for gi in (range(ng) if forward else reversed(range(ng))):
            r0 = gi * G
            b = _dot01(tri01, g_ref[r0:r0 + G, kc_])
            b3 = b.reshape(cpg, L, dk)
            bl3 = b3[:, L - 1:L, :] if forward else b3[:, 0:1, :]
            bl = jnp.broadcast_to(bl3, (cpg, L, dk)).reshape(G, dk)
            pair_rows = [bl3[2 * m] + bl3[2 * m + 1] for m in range(cpg // 2)]
            pair_rows += [jnp.zeros((SUBLANES - cpg // 2, dk), F32)]
            decay_cols = jnp.exp(jnp.concatenate(pair_rows, axis=0).T)
            kc = k_ref[r0:r0 + G, kc_]
            qef = q_ref[r0:r0 + G, kc_] * scale * jnp.exp(b)
            kdf = kc * jnp.exp(bl - b)
            qe = qef.astype(BF16)
            ke = (kc * jnp.exp(-b)).astype(BF16)
            kd = kdf.astype(BF16)
            vb = v_ref[r0:r0 + G, vc_].astype(BF16)
            a = jnp.where(tri, lax.dot_general(qe, ke, NT_DIMS, preferred_element_type=F32), 0.0)
            a = a + jnp.where(pair, lax.dot_general(qe, kd, NT_DIMS, preferred_element_type=F32), 0.0)
            o_intra = jnp.dot(a.astype(BF16), vb, preferred_element_type=F32)
            q_parts, k_parts = [], []
            for c in range(cpg):
                cs = slice(c * L, (c + 1) * L)
                is_first = (c % 2 == 0) == forward
                other = c + 1 if c % 2 == 0 else c - 1
                q_parts.append(qef[cs] if is_first else qef[cs] * jnp.exp(bl3[other]))
                k_parts.append(kdf[cs] * jnp.exp(bl3[other]) if is_first else kdf[cs])
            qs = jnp.concatenate(q_parts, axis=0).astype(BF16)
            ks = jnp.concatenate(k_parts, axis=0).astype(BF16)
            for m in (range(cpg // 2) if forward else reversed(range(cpg // 2))):
                ps = slice(2 * m * L, (2 * m + 2) * L)
                st = st_ref[hh]
                o_ref[r0 + 2 * m * L:r0 + (2 * m + 2) * L, vc_] = o_intra[ps] + jnp.dot(
                    qs[ps], st.astype(BF16), preferred_element_type=F32)
                st_ref[hh] = st * decay_cols[:, m:m + 1] + lax.dot_general(
                    ks[ps], vb[ps], TN_DIMS, preferred_element_type=F32)
        if s_out_ref is not None:
            if fill_slots:
                s_out_ref[0, hh] = st_ref[hh]
                for slot in range(1, s_out_ref.shape[0]):
                    s_out_ref[slot, hh] = jnp.zeros((dk, dv), F32)
            else:
                s_out_ref[hh] = st_ref[hh]

    for hh in range(hps):
        run(True, hh, gf_ref, sf0_ref, sf_ref, stf_ref, of_ref)
        run(False, hh, gb_ref, sb0_ref, sb_ref, stb_ref, ob_ref)

    for hh in range(hps):
        vc_ = slice(hh * dv, (hh + 1) * dv)
        o = of_ref[:, vc_] + ob_ref[:, vc_]
        on = o * lax.rsqrt(jnp.mean(o * o, axis=-1, keepdims=True) + EPS) * gn_ref[...]
        y_ref[:, vc_] = (on * _silu(r_ref[:, vc_])).astype(y_ref.dtype)


GLA_HEADS_PER_STEP = 2


def _gla_heads_per_step(dims):
    hps = GLA_HEADS_PER_STEP
    ok = dims["gh"] % hps == 0 and all(
        dims[o] % (hps * dims[w]) == 0 for o, w in (("o_gq", "dk"), ("o_gk", "dk"), ("o_gv", "dv"), ("o_gr", "dv")))
    return hps if ok else 1


def _gla_common_specs(l, dims, n, rb0, hps):
    dk, dv = dims["dk"] * hps, dims["dv"] * hps
    qb0, kb0, vb0, rb_ = dims["o_gq"] // dk, dims["o_gk"] // dk, dims["o_gv"] // dv, dims["o_gr"] // dv
    return [pl.BlockSpec((n, dk), lambda b, h: (rb0 + b, qb0 + h)),
            pl.BlockSpec((n, dk), lambda b, h: (rb0 + b, kb0 + h)),
            pl.BlockSpec((n, dv), lambda b, h: (rb0 + b, vb0 + h)),
            pl.BlockSpec((n, dv), lambda b, h: (rb0 + b, rb_ + h)),
            pl.BlockSpec((n, LANES), lambda b, h: (rb0 + b, 0)),
            pl.BlockSpec((None, 2, LANES, dk), lambda b, h: (l, 0, 0, h)),
            pl.BlockSpec((None, 2, dk), lambda b, h: (l, 0, h)),
            pl.BlockSpec((None, 1, dims["dv"]), lambda b, h: (l, 0, 0))]


def _gla_scratch(n, dk, dv, hps):
    return [pltpu.VMEM((hps, dk, dv), F32), pltpu.VMEM((hps, dk, dv), F32),
            pltpu.VMEM((n, hps * dv), F32), pltpu.VMEM((n, hps * dv), F32),
            pltpu.VMEM((n, hps * dk), F32), pltpu.VMEM((n, hps * dk), F32)]


def _gla_ctx(p, plr, w2p, gla_b, gla_norm, sf_prev, sb_prev, l, dims):
    n, nseq, gh, dk, dv, depth = dims["s"], dims["b"], dims["gh"], dims["dk"], dims["dv"], dims["depth"]
    hps = _gla_heads_per_step(dims)
    in_specs = _gla_common_specs(l, dims, n, 0, hps)
    args = [p, p, p, p, plr, w2p, gla_b, gla_norm]
    st_shape = jax.ShapeDtypeStruct((nseq, depth, gh, dk, dv), F32)
    first = sf_prev is None
    aliases = {}
    if first:
        st_spec = pl.BlockSpec((None, depth, hps, dk, dv), lambda b, h: (b, 0, h, 0, 0))
    else:
        in_specs += [_any_spec(), _any_spec()]
        args += [sf_prev, sb_prev]
        aliases = {8: 1, 9: 2}
        st_spec = pl.BlockSpec((None, None, hps, dk, dv), lambda b, h: (b, l, h, 0, 0))
    return pl.pallas_call(
        functools.partial(_gla_kernel, n=n, dk=dk, dv=dv, hps=hps, latent=False, fill_slots=first),
        out_shape=[jax.ShapeDtypeStruct((nseq * n, gh * dv), BF16), st_shape, st_shape],
        grid=(nseq, gh // hps), in_specs=in_specs,
        out_specs=[pl.BlockSpec((n, hps * dv), lambda b, h: (b, h)), st_spec, st_spec],
        scratch_shapes=_gla_scratch(n, dk, dv, hps), input_output_aliases=aliases,
        compiler_params=_params(("arbitrary", "arbitrary"), 40), name="gla_ctx",
    )(*args)


def _gla_lat(p, plr, w2p, gla_b, gla_norm, state_f, state_b, l, dims):
    n, nseq, gh, dk, dv = dims["sd"], dims["bd"], dims["gh"], dims["dk"], dims["dv"]
    hps = _gla_heads_per_step(dims)
    rb0 = dims["row_lat"] // n
    in_specs = _gla_common_specs(l, dims, n, rb0, hps)
    s_spec = pl.BlockSpec((None, None, hps, dk, dv), lambda b, h: (b, l, h, 0, 0))
    in_specs += [s_spec, s_spec]
    return pl.pallas_call(
        functools.partial(_gla_kernel, n=n, dk=dk, dv=dv, hps=hps, latent=True, fill_slots=False),
        out_shape=jax.ShapeDtypeStruct((nseq * n, gh * dv), BF16),
        grid=(nseq, gh // hps), in_specs=in_specs,
        out_specs=pl.BlockSpec((n, hps * dv), lambda b, h: (b, h)),
        scratch_shapes=_gla_scratch(n, dk, dv, hps),
        compiler_params=_params(("arbitrary", "arbitrary"), 56), name="gla_lat",
    )(p, p, p, p, plr, w2p, gla_b, gla_norm, state_f, state_b)


SEQ_TILE_ELEMS = 1 << 20


def _seq_col_tile(n, d):
    return _pick(d, tuple(c for c in (4096, 2048, 1024, 512, 256, 128) if n * c <= SEQ_TILE_ELEMS))


def _router_kernel(*refs, n, n_exp, cap):
    lg_ref, h_ref = refs[:2]
    xs_ref, g_ref, pg_ref, colb_ref, afft_ref = refs[-5:]
    nt = n // LANES

    @pl.when(pl.program_id(1) == 0)
    def _route():
        lg = lg_ref[...]
        lane = lax.broadcasted_iota(jnp.int32, lg.shape, 1)
        lg = jnp.where(lane < n_exp, lg, -jnp.inf)
        ex = jnp.exp(lg - jnp.max(lg, axis=-1, keepdims=True))
        aff = ex / jnp.sum(ex, axis=-1, keepdims=True)
        afft = aff.T
        for e in range(n_exp):
            afft_ref[e] = afft[e:e + 1, :]
            colb_ref[e] = jnp.broadcast_to(aff[:, e:e + 1], (n, LANES))
        si = lax.broadcasted_iota(jnp.int32, (LANES, LANES), 0)
        ti = lax.broadcasted_iota(jnp.int32, (LANES, LANES), 1)
        earlier = si < ti
        slot = lax.broadcasted_iota(jnp.int32, (cap, n), 0).astype(F32)

        def per_expert(e, carry):
            parts = []
            for tt in range(nt):
                row = afft_ref[e, :, tt * LANES:(tt + 1) * LANES]
                acc = jnp.zeros((LANES, LANES), F32)
                for c in range(nt):
                    col = colb_ref[e, c * LANES:(c + 1) * LANES, :]
                    if c < tt:
                        ahead = jnp.where(col >= row, 1.0, 0.0)
                    elif c > tt:
                        ahead = jnp.where(col > row, 1.0, 0.0)
                    else:
                        ahead = jnp.where(earlier, jnp.where(col >= row, 1.0, 0.0), jnp.where(col > row, 1.0, 0.0))
                    acc = acc + ahead
                parts.append(jnp.sum(acc, axis=0, keepdims=True))
            rank = jnp.concatenate(parts, axis=1)
            sel = rank == slot
            pg_ref[pl.ds(pl.multiple_of(e * cap, cap), cap), :] = jnp.where(sel, 1.0, 0.0).astype(BF16)
            g_ref[e] = jnp.sum(jnp.where(sel, afft_ref[e], 0.0), axis=1, keepdims=True)
            return carry

        lax.fori_loop(0, n_exp, per_expert, 0, unroll=4 if nt <= 2 else 1)

    xs = jnp.dot(pg_ref[...], h_ref[...], preferred_element_type=F32)
    for e in range(n_exp):
        xs_ref[e] = xs[e * cap:(e + 1) * cap, :].astype(xs_ref.dtype)


def _router(logits, h2, n, nseq, row0, n_exp):
    t, d = h2.shape
    cap = EC_FACTOR * n // n_exp
    td = _seq_col_tile(n, d)
    rb0 = row0 // n
    return pl.pallas_call(
        functools.partial(_router_kernel, n=n, n_exp=n_exp, cap=cap),
        out_shape=[jax.ShapeDtypeStruct((n_exp, nseq * cap, d), BF16),
                   jax.ShapeDtypeStruct((n_exp, nseq * cap, 1), F32),
                   jax.ShapeDtypeStruct((nseq, n_exp * cap, n), BF16)],
        grid=(nseq, d // td),
        in_specs=[pl.BlockSpec((n, LANES), lambda b, j: (rb0 + b, 0)),
                  pl.BlockSpec((n, td), lambda b, j: (rb0 + b, j))],
        out_specs=[pl.BlockSpec((n_exp, cap, td), lambda b, j: (0, b, j)),
                   pl.BlockSpec((n_exp, cap, 1), lambda b, j: (0, b, 0)),
                   pl.BlockSpec((None, n_exp * cap, n), lambda b, j: (b, 0, 0))],
        scratch_shapes=[pltpu.VMEM((n_exp, n, LANES), F32), pltpu.VMEM((n_exp, 1, n), F32)],
        compiler_params=_params(("arbitrary", "arbitrary"), 48), name="router",
    )(logits, h2)


def _scatter_kernel(pg_ref, ye_ref, x_ref, ga_ref, o_ref, *, n_exp):
    ye = jnp.concatenate([ye_ref[e] for e in range(n_exp)], axis=0)
    y = lax.dot_general(pg_ref[...], ye, TN_DIMS, preferred_element_type=F32)
    o_ref[...] = x_ref[...] + ga_ref[...] * y


def _scatter(pg, ye, x1_part, modrows, ga_off, n, row0, slot0):
    rows, d = x1_part.shape
    nseq = rows // n
    n_exp = ye.shape[0]
    cap = pg.shape[1] // n_exp
    td = _seq_col_tile(n, d)
    sb0 = slot0 // cap
    mb0 = row0 // MOD_BLK
    mstep = n // MOD_BLK
    gb = ga_off // td
    return pl.pallas_call(
        functools.partial(_scatter_kernel, n_exp=n_exp),
        out_shape=jax.ShapeDtypeStruct((rows, d), F32), grid=(nseq, d // td),
        in_specs=[pl.BlockSpec((None, n_exp * cap, n), lambda b, j: (b, 0, 0)),
                  pl.BlockSpec((n_exp, cap, td), lambda b, j: (0, sb0 + b, j)),
                  pl.BlockSpec((n, td), lambda b, j: (b, j)),
                  pl.BlockSpec((None, 1, td), lambda b, j: (mb0 + b * mstep, 0, gb + j))],
        out_specs=pl.BlockSpec((n, td), lambda b, j: (b, j)),
        compiler_params=_params(("arbitrary", "arbitrary"), 48), name="scatter",
    )(pg, ye, x1_part, modrows)


def _rope_tables(n, hd):
    qtr = hd // 4
    rows = n // GRID_W
    rpos = jnp.repeat(jnp.arange(rows, dtype=jnp.int32), GRID_W)
    cpos = jnp.arange(rows * GRID_W, dtype=jnp.int32) % GRID_W
    inv = ROPE_THETA ** (-jnp.arange(qtr, dtype=F32) / qtr)
    ar = rpos.astype(F32)[:, None] * inv[None, :]
    ac = cpos.astype(F32)[:, None] * inv[None, :]
    cos = jnp.concatenate([jnp.cos(ar), jnp.cos(ar), jnp.cos(ac), jnp.cos(ac)], axis=-1)
    sin = jnp.concatenate([-jnp.sin(ar), jnp.sin(ar), -jnp.sin(ac), jnp.sin(ac)], axis=-1)
    return cos, sin


def kernel(x_prompt, x_sample, cache_k, cache_v, state_gla_fwd, state_gla_bwd, c, c_ctx, w_mod, b_mod, norm1, norm2, w_in, conv_w, q_norm, k_norm, gla_w2, gla_b, gla_norm, w_conv_out, w_att_out, w_gla_out, w_o, w_router, w_gate, w_up, w_down):
    b, s, d = x_prompt.shape
    bd, sd, _ = x_sample.shape
    depth = w_mod.shape[0]
    cw = conv_w.shape[-1]
    hd = q_norm.shape[-1]
    nkv = cache_k.shape[3]
    aw = w_att_out.shape[1]
    groups = aw // hd // nkv
    gh, dk, dv = state_gla_fwd.shape[2:]
    rank = gla_w2.shape[2]
    n_exp = w_router.shape[-1]
    t = b * s + bd * sd
    row_lat = b * s

    n_conv = 3 * cw
    o_aq = 0
    o_ak = o_aq + aw
    o_av = o_ak + nkv * hd
    o_gq = o_av + nkv * hd
    o_gk = o_gq + gh * dk
    o_gv = o_gk + gh * dk
    o_gr = o_gv + gh * dv
    n_mix = o_gr + gh * dv
    o_lr = n_conv + n_mix
    o_m = o_lr + 2 * rank
    assert s % MOD_BLK == 0 and sd % MOD_BLK == 0 and sd % GRID_W == 0 and s % LANES == 0
    assert row_lat % sd == 0 and 2 * rank <= LANES and o_lr + LANES <= w_in.shape[-1]
    dims = dict(t=t, s=s, sd=sd, b=b, bd=bd, nkv=nkv, groups=groups, hd=hd, gh=gh, dk=dk, dv=dv, depth=depth,
                row_lat=row_lat, o_aq=o_aq, o_ak=o_ak, o_av=o_av, o_gq=o_gq, o_gk=o_gk, o_gv=o_gv, o_gr=o_gr,
                o_lr=o_lr)

    cap_c = EC_FACTOR * s // n_exp
    cap_l = EC_FACTOR * sd // n_exp
    mx = b * cap_c + bd * cap_l
    assert (b * cap_c) % cap_l == 0

    cond8 = jnp.zeros((8, d), F32).at[0].set(c_ctx).at[1:1 + bd].set(c)
    mod_all = _modulation(cond8, w_mod, b_mod)
    blk_row = np.concatenate([np.zeros(b * s // MOD_BLK, np.int32),
                              np.repeat(np.arange(1, bd + 1, dtype=np.int32), sd // MOD_BLK)])

    x_ctx = x_prompt.reshape(b * s, d)
    x_lat = x_sample.reshape(bd * sd, d)
    cos, sin = _rope_tables(sd, hd)
    w2p = jnp.zeros((depth, 2, LANES, gh * dk), F32)
    w2p = w2p.at[:, 0, :rank].set(gla_w2[:, 0]).at[:, 1, rank:2 * rank].set(gla_w2[:, 1])
    wr_pad = jnp.zeros((depth, d, LANES), F32).at[:, :, :n_exp].set(w_router)
    q_norm3 = q_norm.reshape(depth, 1, hd)
    k_norm3 = k_norm.reshape(depth, 1, hd)
    gla_norm3 = gla_norm.reshape(depth, 1, dv)
    cache_k4 = cache_k.reshape(bd, depth, cache_k.shape[2], nkv * hd)
    cache_v4 = cache_v.reshape(bd, depth, cache_v.shape[2], nkv * hd)
    w_in_t = jnp.swapaxes(w_in, 1, 2)
    w_conv_out_b, w_att_out_b, w_gla_out_b = (w.astype(BF16) for w in (w_conv_out, w_att_out, w_gla_out))

    new_k, new_v = [], []
    sf = sb = None
    for l in range(depth):
        modrows = mod_all[l][blk_row][:, None, :]
        (h,) = _normmod(x_ctx, x_lat, modrows, norm1[l], 0, d)
        p_conv = _proj(h, w_in_t, l, 0, n_conv, BF16, _epi_identity, "conv_proj")
        p = _proj(h, w_in_t, l, n_conv, n_mix, F32, _epi_identity, "in_proj")
        plr = _proj(h, w_in_t, l, o_lr, LANES, F32, _epi_identity, "lr_proj", tn=LANES)

        y_conv = _conv(p_conv, conv_w, l, t, cw, s, sd, row_lat)

        y_att_c, k_new = _attn_ctx(p, q_norm3, k_norm3, l, dims)
        y_att_l = _attn_lat(p, q_norm3, k_norm3, cos, sin, cache_k4, cache_v4, l, dims)
        new_k.append(k_new.reshape(b, s, nkv, hd))
        new_v.append(p[:row_lat, o_av:o_av + nkv * hd].reshape(b, s, nkv, hd))

        y_gla_c, sf, sb = _gla_ctx(p, plr, w2p, gla_b, gla_norm3, sf, sb, l, dims)
        y_gla_l = _gla_lat(p, plr, w2p, gla_b, gla_norm3, state_gla_fwd, state_gla_bwd, l, dims)

        gates = _proj(h, w_in_t, l, o_m, 3 * d, BF16, _epi_sigmoid, "gate_proj")
        out_w = (w_conv_out_b, w_att_out_b, w_gla_out_b)
        merged_c = _merge(y_conv, y_att_c, y_gla_c, gates, *out_w, l, d, 0)
        merged_l = _merge(y_conv, y_att_l, y_gla_l, gates, *out_w, l, d, row_lat)
        x1_ctx = _out_proj(merged_c, w_o, l, x_ctx, 0, modrows, 2 * d)
        x1_lat = _out_proj(merged_l, w_o, l, x_lat, row_lat, modrows, 2 * d)

        h2, logits = _normmod(x1_ctx, x1_lat, modrows, norm2[l], 3 * d, 4 * d, w_router_pad=wr_pad[l])
        xs_c, g_c, pg_c = _router(logits, h2, s, b, 0, n_exp)
        xs_l, g_l, pg_l = _router(logits, h2, sd, bd, row_lat, n_exp)
        hid = _ffn_up((xs_c, xs_l), w_gate, w_up, l)
        ye = _ffn_down(hid, w_down, (g_c, g_l), l)
        x_ctx = _scatter(pg_c, ye, x1_ctx, modrows, 5 * d, s, 0, 0)
        x_lat = _scatter(pg_l, ye, x1_lat, modrows, 5 * d, sd, row_lat, b * cap_c)

    y_prompt = x_ctx.reshape(b, s, d)
    y_sample = x_lat.reshape(bd, sd, d)
    return (y_prompt, y_sample, jnp.stack(new_k, axis=1), jnp.stack(new_v, axis=1), sf, sb)
```

```python
import functools

import jax
import jax.numpy as jnp
import numpy as np
from jax import lax
from jax.experimental import pallas as pl
from jax.experimental.pallas import tpu as pltpu

F32 = jnp.float32
BF16 = jnp.bfloat16

GRID_W = 64
GLA_CHUNK = 64
GLA_TAU = 16.0
EC_FACTOR = 2
ROPE_THETA = 10000.0
EPS = 1e-6
LOG2_E = 1.4426950408889634

LANES = 128
SUBLANES = 8
MOD_BLK = 256
GLA_GROUP = 256
NT_DIMS = (((1,), (1,)), ((), ()))
TN_DIMS = (((0,), (0,)), ((), ()))


def _params(sem, vmem_mb):
    return pltpu.CompilerParams(dimension_semantics=sem, vmem_limit_bytes=vmem_mb << 20)


def _any_spec():
    return pl.BlockSpec(memory_space=pl.ANY)


def _sigmoid(x):
    return 0.5 * jnp.tanh(0.5 * x) + 0.5


def _silu(x):
    return x * _sigmoid(x)


def _dot_split(a, b):
    a1 = a.astype(BF16)
    a2 = (a - a1.astype(F32)).astype(BF16)
    b1 = b.astype(BF16)
    b2 = (b - b1.astype(F32)).astype(BF16)
    return (jnp.dot(a1, b1, preferred_element_type=F32) + jnp.dot(a1, b2, preferred_element_type=F32)
            + jnp.dot(a2, b1, preferred_element_type=F32))


def _pick(total, candidates):
    for c in candidates:
        if total % c == 0:
            return c
    raise ValueError(f"no tile in {candidates} divides {total}")


def _mod_kernel(c_ref, w_ref, b_ref, o_ref):
    s = _silu(c_ref[...]).astype(BF16)
    o_ref[...] = jnp.dot(s, w_ref[...].astype(BF16), preferred_element_type=F32) + b_ref[...]


def _modulation(cond8, w_mod, b_mod):
    depth, d, n6 = w_mod.shape
    tn = _pick(n6, (512, 256, 128))
    return pl.pallas_call(
        _mod_kernel,
        out_shape=jax.ShapeDtypeStruct((depth, 8, n6), F32),
        grid=(depth, n6 // tn),
        in_specs=[pl.BlockSpec((8, d), lambda l, j: (0, 0)),
                  pl.BlockSpec((None, d, tn), lambda l, j: (l, 0, j)),
                  pl.BlockSpec((None, 1, tn), lambda l, j: (l, 0, j))],
        out_specs=pl.BlockSpec((None, 8, tn), lambda l, j: (l, 0, j)),
        compiler_params=_params(("arbitrary", "arbitrary"), 40),
        name="modulation",
    )(cond8, w_mod, b_mod.reshape(depth, 1, n6))


def _normmod_kernel(xc_ref, xl_ref, m_ref, g_ref, *rest, d, sh_off, sc_off, router, ctx_blocks):
    def emit(x_ref):
        x = x_ref[...]
        y = x * lax.rsqrt(jnp.mean(x * x, axis=-1, keepdims=True) + EPS) * g_ref[...]
        h = y * (1.0 + m_ref[:, sc_off:sc_off + d]) + m_ref[:, sh_off:sh_off + d]
        if router:
            wr_ref, o_ref, lg_ref = rest
            lg_ref[...] = _dot_split(h, wr_ref[...])
        else:
            (o_ref,) = rest
        o_ref[...] = h.astype(BF16)

    pl.when(pl.program_id(0) < ctx_blocks)(lambda: emit(xc_ref))
    pl.when(pl.program_id(0) >= ctx_blocks)(lambda: emit(xl_ref))


def _normmod(x_ctx, x_lat, modrows, gain, sh_off, sc_off, w_router_pad=None):
    (rc, d), rl = x_ctx.shape, x_lat.shape[0]
    router = w_router_pad is not None
    nc, nl = rc // MOD_BLK, rl // MOD_BLK
    in_specs = [pl.BlockSpec((MOD_BLK, d), lambda i: (jnp.minimum(i, nc - 1), 0)),
                pl.BlockSpec((MOD_BLK, d), lambda i: (jnp.maximum(i - nc, 0), 0)),
                pl.BlockSpec((None, 1, 6 * d), lambda i: (i, 0, 0)),
                pl.BlockSpec((1, d), lambda i: (0, 0))]
    args = [x_ctx, x_lat, modrows, gain.reshape(1, d)]
    out_shape = [jax.ShapeDtypeStruct((rc + rl, d), BF16)]
    out_specs = [pl.BlockSpec((MOD_BLK, d), lambda i: (i, 0))]
    if router:
        in_specs.append(pl.BlockSpec((d, LANES), lambda i: (0, 0)))
        args.append(w_router_pad)
        out_shape.append(jax.ShapeDtypeStruct((rc + rl, LANES), F32))
        out_specs.append(pl.BlockSpec((MOD_BLK, LANES), lambda i: (i, 0)))
    return pl.pallas_call(
        functools.partial(_normmod_kernel, d=d, sh_off=sh_off, sc_off=sc_off, router=router, ctx_blocks=nc),
        out_shape=out_shape, grid=(nc + nl,), in_specs=in_specs, out_specs=out_specs,
        compiler_params=_params(("arbitrary",), 48),
        name="normmod_router" if router else "normmod",
    )(*args)


def _mm_kernel(*refs, n_a, n_w, n_sides, w_rows, row_parts, epilogue):
    a_refs = refs[:n_a]
    w_refs = refs[n_a:n_a + n_w]
    side_refs = refs[n_a + n_w:n_a + n_w + n_sides]
    o_ref = refs[n_a + n_w + n_sides]
    wbs = [(w_ref[0] if w_rows else w_ref[...]).astype(BF16) for w_ref in w_refs]

    def dot(a, wb):
        if w_rows:
            return lax.dot_general(a, wb, NT_DIMS, preferred_element_type=F32)
        return jnp.dot(a, wb, preferred_element_type=F32)

    sides = [s[...] for s in side_refs]
    if row_parts:
        r0 = 0
        for a_ref in a_refs:
            r1 = r0 + a_ref.shape[0]
            o_ref[r0:r1, :] = epilogue([dot(a_ref[...], wb) for wb in wbs], sides).astype(o_ref.dtype)
            r0 = r1
    else:
        accs = [dot(a_refs[q if n_a > 1 else 0][...], wb) for q, wb in enumerate(wbs)]
        o_ref[...] = epilogue(accs, sides).astype(o_ref.dtype)


def _fused_mm(grid, a_args, w_args, side_args, out_shape, out_spec, epilogue, vmem_mb, name, w_rows=False,
              row_parts=False):
    assert row_parts or len(a_args) in (1, len(w_args))
    arrays = [a for a, _ in a_args] + [w for w, _ in w_args] + [s for s, _ in side_args]
    specs = [s for _, s in a_args] + [s for _, s in w_args] + [s for _, s in side_args]
    body = functools.partial(_mm_kernel, n_a=len(a_args), n_w=len(w_args), n_sides=len(side_args),
                             w_rows=w_rows, row_parts=row_parts, epilogue=epilogue)
    return pl.pallas_call(
        body, out_shape=out_shape, grid=grid, in_specs=specs, out_specs=out_spec,
        compiler_params=_params(("arbitrary",) * len(grid), vmem_mb), name=name,
    )(*arrays)


def _epi_identity(accs, sides):
    return accs[0]


def _epi_sigmoid(accs, sides):
    return _sigmoid(accs[0])


def _epi_residual(accs, sides):
    x, ga = sides
    return x + ga * accs[0]


def _epi_swiglu(accs, sides):
    return _silu(accs[0]) * accs[1]


def _epi_merge(accs, sides):
    return sides[0] * accs[0] + sides[1] * accs[1] + sides[2] * accs[2]


def _epi_rowscale(accs, sides):
    return accs[0] * jnp.concatenate(sides, axis=0)


def _proj(h, w_t, l, row0, n_cols, out_dtype, epilogue, name, tn=512):
    t, k = h.shape
    tm = _pick(t, (1536, 1024, 512, 256))
    assert row0 % SUBLANES == 0 and n_cols % tn == 0
    elem = (pl.Element(1), pl.Element(tn), pl.Element(k))
    return _fused_mm(
        (t // tm, n_cols // tn),
        [(h, pl.BlockSpec((tm, k), lambda i, j: (i, 0)))],
        [(w_t, pl.BlockSpec(elem, lambda i, j: (l, pl.multiple_of(row0 + j * tn, SUBLANES), 0)))],
        [], jax.ShapeDtypeStruct((t, n_cols), out_dtype),
        pl.BlockSpec((tm, tn), lambda i, j: (i, j)), epilogue, 58, name, w_rows=True)


def _merge(y_conv, y_att, y_gla, gates, w_conv_out, w_att_out, w_gla_out, l, d, row0):
    rows = y_att.shape[0]
    tm = _pick(rows, (1024, 512, 256))
    tn = _pick(d, (512, 256, 128))
    assert row0 % tm == 0
    ib0 = row0 // tm
    nb = d // tn
    ws = (w_conv_out, w_att_out, w_gla_out)
    a_args = [(y_conv, pl.BlockSpec((tm, y_conv.shape[1]), lambda i, j: (ib0 + i, 0))),
              (y_att, pl.BlockSpec((tm, y_att.shape[1]), lambda i, j: (i, 0))),
              (y_gla, pl.BlockSpec((tm, y_gla.shape[1]), lambda i, j: (i, 0)))]
    w_args = [(w, pl.BlockSpec((None, w.shape[1], tn), lambda i, j: (l, 0, j))) for w in ws]
    side_args = [(gates, pl.BlockSpec((tm, tn), functools.partial(lambda i, j, q: (ib0 + i, q * nb + j), q=q)))
                 for q in range(3)]
    return _fused_mm((rows // tm, nb), a_args, w_args, side_args,
                     jax.ShapeDtypeStruct((rows, d), BF16), pl.BlockSpec((tm, tn), lambda i, j: (i, j)),
                     _epi_merge, 58, "merge")


def _out_proj(merged, w_o, l, x_part, row0, modrows, ga_off):
    rows, d = x_part.shape
    tm = _pick(rows, (1024, 512, 256))
    tn = _pick(d, (512, 256, 128))
    mb0 = row0 // MOD_BLK
    rb = tm // MOD_BLK
    gb = ga_off // tn
    return _fused_mm(
        (rows // tm, d // tn),
        [(merged, pl.BlockSpec((tm, d), lambda i, j: (i, 0)))],
        [(w_o, pl.BlockSpec((None, d, tn), lambda i, j: (l, 0, j)))],
        [(x_part, pl.BlockSpec((tm, tn), lambda i, j: (i, j))),
         (modrows, pl.BlockSpec((None, 1, tn), lambda i, j: (mb0 + i * rb, 0, gb + j)))],
        jax.ShapeDtypeStruct((rows, d), F32), pl.BlockSpec((tm, tn), lambda i, j: (i, j)),
        _epi_residual, 52, "out_proj")


def _ffn_up(xs_parts, w_gate, w_up, l):
    e, _, d = xs_parts[0].shape
    mx = sum(x.shape[1] for x in xs_parts)
    f = w_gate.shape[-1]
    tf = _pick(f, (256, 128))
    wspec = pl.BlockSpec((None, None, d, tf), lambda ee, j: (l, ee, 0, j))
    return _fused_mm(
        (e, f // tf),
        [(x, pl.BlockSpec((None, x.shape[1], d), lambda ee, j: (ee, 0, 0))) for x in xs_parts],
        [(w_gate, wspec), (w_up, wspec)], [],
        jax.ShapeDtypeStruct((e, mx, f), BF16), pl.BlockSpec((None, mx, tf), lambda ee, j: (ee, 0, j)),
        _epi_swiglu, 58, "ffn_up", row_parts=True)


def _ffn_down(hid, w_down, gate_parts, l):
    e, mx, f = hid.shape
    d = w_down.shape[-1]
    tn = _pick(d, (1024, 512, 256, 128))
    return _fused_mm(
        (e, d // tn),
        [(hid, pl.BlockSpec((None, mx, f), lambda ee, j: (ee, 0, 0)))],
        [(w_down, pl.BlockSpec((None, None, f, tn), lambda ee, j: (l, ee, 0, j)))],
        [(g, pl.BlockSpec((None, g.shape[1], 1), lambda ee, j: (ee, 0, 0))) for g in gate_parts],
        jax.ShapeDtypeStruct((e, mx, d), BF16), pl.BlockSpec((None, mx, tn), lambda ee, j: (ee, 0, j)),
        _epi_rowscale, 56, "ffn_down")


def _conv_kernel(cb_ref, cc_ref, ch_ref, w_ref, o_ref, *, ctx_blocks, s, sd):
    u = cc_ref[...].astype(F32) * ch_ref[...].astype(F32)
    n = u.shape[0]
    row = lax.broadcasted_iota(jnp.int32, u.shape, 0)
    last = jnp.where(pl.program_id(0) < ctx_blocks, s - 1, sd - 1)
    pos = row & last
    prev = jnp.where(pos == 0, 0.0, pltpu.roll(u, 1, 0))
    nxt = jnp.where(pos == last, 0.0, pltpu.roll(u, n - 1, 0))
    z = w_ref[0:1, :] * prev + w_ref[1:2, :] * u + w_ref[2:3, :] * nxt
    o_ref[...] = (cb_ref[...].astype(F32) * z).astype(o_ref.dtype)


def _conv(p, conv_w, l, t, cw, s, sd, row_lat):
    assert s & (s - 1) == 0 and sd & (sd - 1) == 0 and sd % s == 0 and row_lat % sd == 0
    tc = _pick(cw, (1024, 512, 256, 128))
    nb = cw // tc
    return pl.pallas_call(
        functools.partial(_conv_kernel, ctx_blocks=row_lat // sd, s=s, sd=sd),
        out_shape=jax.ShapeDtypeStruct((t, cw), BF16), grid=(t // sd, nb),
        in_specs=[pl.BlockSpec((sd, tc), lambda b, j: (b, j)),
                  pl.BlockSpec((sd, tc), lambda b, j: (b, nb + j)),
                  pl.BlockSpec((sd, tc), lambda b, j: (b, 2 * nb + j)),
                  pl.BlockSpec((None, 3, tc), lambda b, j: (l, 0, j))],
        out_specs=pl.BlockSpec((sd, tc), lambda b, j: (b, j)),
        compiler_params=_params(("arbitrary", "arbitrary"), 40), name="conv",
    )(p, p, p, conv_w)


def _head_norm(x, gain):
    return x * lax.rsqrt(jnp.mean(x * x, axis=-1, keepdims=True) + EPS) * gain


def _rope(x, cos, sin_signed):
    hd = x.shape[-1]
    lane = lax.broadcasted_iota(jnp.int32, x.shape, 1)
    q = hd // 4
    swapped = jnp.where((lane & q) == 0, pltpu.roll(x, hd - q, 1), pltpu.roll(x, q, 1))
    return x * cos + swapped * sin_signed


def _attn_kernel(*refs, nkv, groups, hd, latent):
    if latent:
        (q_ref, k_ref, v_ref, qg_ref, kg_ref, cosq_ref, sinq_ref, cosk_ref, sink_ref,
         ck_ref, cv_ref, y_ref, kb_ref) = refs

        @pl.when(pl.program_id(1) == 0)
        def _keys():
            for kv in range(nkv):
                ks = slice(kv * hd, (kv + 1) * hd)
                kn = _rope(_head_norm(k_ref[:, ks], kg_ref[...]), cosk_ref[...], sink_ref[...])
                kb_ref[:, ks] = kn.astype(BF16)
    else:
        q_ref, k_ref, v_ref, qg_ref, kg_ref, y_ref, kout_ref = refs
    c = (hd ** -0.5) * LOG2_E
    for kv in range(nkv):
        ks = slice(kv * hd, (kv + 1) * hd)
        if latent:
            kb = kb_ref[:, ks]
            ckb = ck_ref[:, ks].astype(BF16)
            cvb = cv_ref[:, ks].astype(BF16)
        else:
            kn = _head_norm(k_ref[:, ks], kg_ref[...])
            kout_ref[:, ks] = kn
            kb = kn.astype(BF16)
        vb = v_ref[:, ks].astype(BF16)
        for g in range(groups):
            qs = slice((kv * groups + g) * hd, (kv * groups + g + 1) * hd)
            qn = _head_norm(q_ref[:, qs], qg_ref[...])
            if latent:
                qn = _rope(qn, cosq_ref[...], sinq_ref[...])
            qb = qn.astype(BF16)
            s = lax.dot_general(qb, kb, NT_DIMS, preferred_element_type=F32)
            m = jnp.max(s, axis=-1, keepdims=True)
            if latent:
                s2 = lax.dot_general(qb, ckb, NT_DIMS, preferred_element_type=F32)
                m = jnp.maximum(m, jnp.max(s2, axis=-1, keepdims=True))
                p2 = jnp.exp2((s2 - m) * c)
            p = jnp.exp2((s - m) * c)
            den = jnp.sum(p, axis=-1, keepdims=True)
            o = jnp.dot(p.astype(BF16), vb, preferred_element_type=F32)
            if latent:
                den = den + jnp.sum(p2, axis=-1, keepdims=True)
                o = o + jnp.dot(p2.astype(BF16), cvb, preferred_element_type=F32)
            y_ref[:, qs] = (o / den).astype(y_ref.dtype)


def _attn_ctx(p, q_norm, k_norm, l, dims):
    t, n, nseq, nkv, groups, hd = dims["t"], dims["s"], dims["b"], dims["nkv"], dims["groups"], dims["hd"]
    aw, kw = nkv * groups * hd, nkv * hd
    assert dims["o_aq"] % aw == 0 and dims["o_ak"] % kw == 0 and dims["o_av"] % kw == 0
    qb0, kb0, vb0 = dims["o_aq"] // aw, dims["o_ak"] // kw, dims["o_av"] // kw
    return pl.pallas_call(
        functools.partial(_attn_kernel, nkv=nkv, groups=groups, hd=hd, latent=False),
        out_shape=[jax.ShapeDtypeStruct((nseq * n, aw), BF16), jax.ShapeDtypeStruct((nseq * n, kw), F32)],
        grid=(nseq,),
        in_specs=[pl.BlockSpec((n, aw), lambda b: (b, qb0)),
                  pl.BlockSpec((n, kw), lambda b: (b, kb0)),
                  pl.BlockSpec((n, kw), lambda b: (b, vb0)),
                  pl.BlockSpec((None, 1, hd), lambda b: (l, 0, 0)),
                  pl.BlockSpec((None, 1, hd), lambda b: (l, 0, 0))],
        out_specs=[pl.BlockSpec((n, aw), lambda b: (b, 0)),
                   pl.BlockSpec((n, kw), lambda b: (b, 0))],
        compiler_params=_params(("arbitrary",), 40), name="attn_ctx",
    )(p, p, p, q_norm, k_norm)


def _attn_lat(p, q_norm, k_norm, cos, sin, cache_k, cache_v, l, dims):
    t, n, nseq, nkv, groups, hd = dims["t"], dims["sd"], dims["bd"], dims["nkv"], dims["groups"], dims["hd"]
    past = cache_k.shape[2]
    aw, kw = nkv * groups * hd, nkv * hd
    tq = _pick(n, (256, 128))
    nq = n // tq
    qb0, kb0, vb0 = dims["o_aq"] // aw, dims["o_ak"] // kw, dims["o_av"] // kw
    rq0 = dims["row_lat"] // tq
    rk0 = dims["row_lat"] // n
    return pl.pallas_call(
        functools.partial(_attn_kernel, nkv=nkv, groups=groups, hd=hd, latent=True),
        out_shape=jax.ShapeDtypeStruct((nseq * n, aw), BF16),
        grid=(nseq, nq),
        in_specs=[pl.BlockSpec((tq, aw), lambda b, i: (rq0 + b * nq + i, qb0)),
                  pl.BlockSpec((n, kw), lambda b, i: (rk0 + b, kb0)),
                  pl.BlockSpec((n, kw), lambda b, i: (rk0 + b, vb0)),
                  pl.BlockSpec((None, 1, hd), lambda b, i: (l, 0, 0)),
                  pl.BlockSpec((None, 1, hd), lambda b, i: (l, 0, 0)),
                  pl.BlockSpec((tq, hd), lambda b, i: (i, 0)),
                  pl.BlockSpec((tq, hd), lambda b, i: (i, 0)),
                  pl.BlockSpec((n, hd), lambda b, i: (0, 0)),
                  pl.BlockSpec((n, hd), lambda b, i: (0, 0)),
                  pl.BlockSpec((None, None, past, kw), lambda b, i: (b, l, 0, 0)),
                  pl.BlockSpec((None, None, past, kw), lambda b, i: (b, l, 0, 0))],
        out_specs=pl.BlockSpec((tq, aw), lambda b, i: (b * nq + i, 0)),
        scratch_shapes=[pltpu.VMEM((n, kw), BF16)],
        compiler_params=_params(("arbitrary",) * 2, 48), name="attn_lat",
    )(p, p, p, q_norm, k_norm, cos, sin, cos, sin, cache_k, cache_v)


def _log_sigmoid(z):
    return jnp.minimum(z, 0.0) - jnp.log1p(jnp.exp(-jnp.abs(z)))


def _dot01(m01, x):
    x1 = x.astype(BF16)
    r1 = x - x1.astype(F32)
    x2 = r1.astype(BF16)
    x3 = (r1 - x2.astype(F32)).astype(BF16)
    return (jnp.dot(m01, x1, preferred_element_type=F32) + jnp.dot(m01, x2, preferred_element_type=F32)
            + jnp.dot(m01, x3, preferred_element_type=F32))


def _gla_kernel(*refs, n, dk, dv, hps, latent, fill_slots):
    q_ref, k_ref, v_ref, r_ref, lr_ref, w2_ref, b_ref, gn_ref = refs[:8]
    stf_ref, stb_ref, of_ref, ob_ref, gf_ref, gb_ref = refs[-6:]
    if latent:
        sf0_ref, sb0_ref, y_ref = refs[8:11]
        sf_ref = sb_ref = None
    else:
        y_ref, sf_ref, sb_ref = refs[-9:-6]
        sf0_ref = sb0_ref = None
    L = GLA_CHUNK
    G = min(GLA_GROUP, n)
    cpg = G // L
    ng = n // G
    scale = dk ** -0.5

    lr = lr_ref[...]
    for d, g_ref in ((0, gf_ref), (1, gb_ref)):
        z = _dot_split(lr, w2_ref[d]) + b_ref[d:d + 1, :]
        g_ref[...] = _log_sigmoid(z) / GLA_TAU

    ri = lax.broadcasted_iota(jnp.int32, (G, G), 0)
    ci = lax.broadcasted_iota(jnp.int32, (G, G), 1)
    same_chunk = (ri // L) == (ci // L)

    def run(forward, hh, g_ref, s0_ref, s_out_ref, st_ref, o_ref):
        kc_ = slice(hh * dk, (hh + 1) * dk)
        vc_ = slice(hh * dv, (hh + 1) * dv)
        tri = same_chunk & ((ri >= ci) if forward else (ri <= ci))
        tri01 = jnp.where(tri, 1.0, 0.0).astype(BF16)
        if forward:
            pair = (ri // L == ci // L + 1) & ((ci // L) % 2 == 0)
        else:
            pair = (ri // L == ci // L - 1) & ((ri // L) % 2 == 0)
        st_ref[hh] = jnp.zeros((dk, dv), F32) if s0_ref is None else s0_ref[hh]
        for gi in (range(ng) if forward else reversed(range(ng))):
            r0 = gi * G
            b = _dot01(tri01, g_ref[r0:r0 + G, kc_])
            b3 = b.reshape(cpg, L, dk)
            bl3 = b3[:, L - 1:L, :] if forward else b3[:, 0:1, :]
            bl = jnp.broadcast_to(bl3, (cpg, L, dk)).reshape(G, dk)
            pair_rows = [bl3[2 * m] + bl3[2 * m + 1] for m in range(cpg // 2)]
            pair_rows += [jnp.zeros((SUBLANES - cpg // 2, dk), F32)]
            decay_cols = jnp.exp(jnp.concatenate(pair_rows, axis=0).T)
            kc = k_ref[r0:r0 + G, kc_]
            qef = q_ref[r0:r0 + G, kc_] * scale * jnp.exp(b)
            kdf = kc * jnp.exp(bl - b)
            qe = qef.astype(BF16)
            ke = (kc * jnp.exp(-b)).astype(BF16)
            kd = kdf.astype(BF16)
            vb = v_ref[r0:r0 + G, vc_].astype(BF16)
            a = jnp.where(tri, lax.dot_general(qe, ke, NT_DIMS, preferred_element_type=F32), 0.0)
            a = a + jnp.where(pair, lax.dot_general(qe, kd, NT_DIMS, preferred_element_type=F32), 0.0)
            o_intra = jnp.dot(a.astype(BF16), vb, preferred_element_type=F32)
            q_parts, k_parts = [], []
            for c in range(cpg):
                cs = slice(c * L, (c + 1) * L)
                is_first = (c % 2 == 0) == forward
                other = c + 1 if c % 2 == 0 else c - 1
                q_parts.append(qef[cs] if is_first else qef[cs] * jnp.exp(bl3[other]))
                k_parts.append(kdf[cs] * jnp.exp(bl3[other]) if is_first else kdf[cs])
            qs = jnp.concatenate(q_parts, axis=0).astype(BF16)
            ks = jnp.concatenate(k_parts, axis=0).astype(BF16)
            for m in (range(cpg // 2) if forward else reversed(range(cpg // 2))):
                ps = slice(2 * m * L, (2 * m + 2) * L)
                st = st_ref[hh]
                o_ref[r0 + 2 * m * L:r0 + (2 * m + 2) * L, vc_] = o_intra[ps] + jnp.dot(
                    qs[ps], st.astype(BF16), preferred_element_type=F32)
                st_ref[hh] = st * decay_cols[:, m:m + 1] + lax.dot_general(
                    ks[ps], vb[ps], TN_DIMS, preferred_element_type=F32)
        if s_out_ref is not None:
            if fill_slots:
                s_out_ref[0, hh] = st_ref[hh]
                for slot in range(1, s_out_ref.shape[0]):
                    s_out_ref[slot, hh] = jnp.zeros((dk, dv), F32)
            else:
                s_out_ref[hh] = st_ref[hh]

    for hh in range(hps):
        run(True, hh, gf_ref, sf0_ref, sf_ref, stf_ref, of_ref)
        run(False, hh, gb_ref, sb0_ref, sb_ref, stb_ref, ob_ref)

    for hh in range(hps):
        vc_ = slice(hh * dv, (hh + 1) * dv)
        o = of_ref[:, vc_] + ob_ref[:, vc_]
        on = o * lax.rsqrt(jnp.mean(o * o, axis=-1, keepdims=True) + EPS) * gn_ref[...]
        y_ref[:, vc_] = (on * _silu(r_ref[:, vc_])).astype(y_ref.dtype)


GLA_HEADS_PER_STEP = 2


def _gla_heads_per_step(dims):
    hps = GLA_HEADS_PER_STEP
    ok = dims["gh"] % hps == 0 and all(
        dims[o] % (hps * dims[w]) == 0 for o, w in (("o_gq", "dk"), ("o_gk", "dk"), ("o_gv", "dv"), ("o_gr", "dv")))
    return hps if ok else 1


def _gla_common_specs(l, dims, n, rb0, hps):
    dk, dv = dims["dk"] * hps, dims["dv"] * hps
    qb0, kb0, vb0, rb_ = dims["o_gq"] // dk, dims["o_gk"] // dk, dims["o_gv"] // dv, dims["o_gr"] // dv
    return [pl.BlockSpec((n, dk), lambda b, h: (rb0 + b, qb0 + h)),
            pl.BlockSpec((n, dk), lambda b, h: (rb0 + b, kb0 + h)),
            pl.BlockSpec((n, dv), lambda b, h: (rb0 + b, vb0 + h)),
            pl.BlockSpec((n, dv), lambda b, h: (rb0 + b, rb_ + h)),
            pl.BlockSpec((n, LANES), lambda b, h: (rb0 + b, 0)),
            pl.BlockSpec((None, 2, LANES, dk), lambda b, h: (l, 0, 0, h)),
            pl.BlockSpec((None, 2, dk), lambda b, h: (l, 0, h)),
            pl.BlockSpec((None, 1, dims["dv"]), lambda b, h: (l, 0, 0))]


def _gla_scratch(n, dk, dv, hps):
    return [pltpu.VMEM((hps, dk, dv), F32), pltpu.VMEM((hps, dk, dv), F32),
            pltpu.VMEM((n, hps * dv), F32), pltpu.VMEM((n, hps * dv), F32),
            pltpu.VMEM((n, hps * dk), F32), pltpu.VMEM((n, hps * dk), F32)]


def _gla_ctx(p, plr, w2p, gla_b, gla_norm, sf_prev, sb_prev, l, dims):
    n, nseq, gh, dk, dv, depth = dims["s"], dims["b"], dims["gh"], dims["dk"], dims["dv"], dims["depth"]
    hps = _gla_heads_per_step(dims)
    in_specs = _gla_common_specs(l, dims, n, 0, hps)
    args = [p, p, p, p, plr, w2p, gla_b, gla_norm]
    st_shape = jax.ShapeDtypeStruct((nseq, depth, gh, dk, dv), F32)
    first = sf_prev is None
    aliases = {}
    if first:
        st_spec = pl.BlockSpec((None, depth, hps, dk, dv), lambda b, h: (b, 0, h, 0, 0))
    else:
        in_specs += [_any_spec(), _any_spec()]
        args += [sf_prev, sb_prev]
        aliases = {8: 1, 9: 2}
        st_spec = pl.BlockSpec((None, None, hps, dk, dv), lambda b, h: (b, l, h, 0, 0))
    return pl.pallas_call(
        functools.partial(_gla_kernel, n=n, dk=dk, dv=dv, hps=hps, latent=False, fill_slots=first),
        out_shape=[jax.ShapeDtypeStruct((nseq * n, gh * dv), BF16), st_shape, st_shape],
        grid=(nseq, gh // hps), in_specs=in_specs,
        out_specs=[pl.BlockSpec((n, hps * dv), lambda b, h: (b, h)), st_spec, st_spec],
        scratch_shapes=_gla_scratch(n, dk, dv, hps), input_output_aliases=aliases,
        compiler_params=_params(("arbitrary", "arbitrary"), 40), name="gla_ctx",
    )(*args)


def _gla_lat(p, plr, w2p, gla_b, gla_norm, state_f, state_b, l, dims):
    n, nseq, gh, dk, dv = dims["sd"], dims["bd"], dims["gh"], dims["dk"], dims["dv"]
    hps = _gla_heads_per_step(dims)
    rb0 = dims["row_lat"] // n
    in_specs = _gla_common_specs(l, dims, n, rb0, hps)
    s_spec = pl.BlockSpec((None, None, hps, dk, dv), lambda b, h: (b, l, h, 0, 0))
    in_specs += [s_spec, s_spec]
    return pl.pallas_call(
        functools.partial(_gla_kernel, n=n, dk=dk, dv=dv, hps=hps, latent=True, fill_slots=False),
        out_shape=jax.ShapeDtypeStruct((nseq * n, gh * dv), BF16),
        grid=(nseq, gh // hps), in_specs=in_specs,
        out_specs=pl.BlockSpec((n, hps * dv), lambda b, h: (b, h)),
        scratch_shapes=_gla_scratch(n, dk, dv, hps),
        compiler_params=_params(("arbitrary", "arbitrary"), 56), name="gla_lat",
    )(p, p, p, p, plr, w2p, gla_b, gla_norm, state_f, state_b)


SEQ_TILE_ELEMS = 1 << 20


def _seq_col_tile(n, d):
    return _pick(d, tuple(c for c in (4096, 2048, 1024, 512, 256, 128) if n * c <= SEQ_TILE_ELEMS))


def _router_kernel(*refs, n, n_exp, cap):
    lg_ref, h_ref = refs[:2]
    xs_ref, g_ref, pg_ref, colb_ref, afft_ref = refs[-5:]
    nt = n // LANES

    @pl.when(pl.program_id(1) == 0)
    def _route():
        lg = lg_ref[...]
        lane = lax.broadcasted_iota(jnp.int32, lg.shape, 1)
        lg = jnp.where(lane < n_exp, lg, -jnp.inf)
        ex = jnp.exp(lg - jnp.max(lg, axis=-1, keepdims=True))
        aff = ex / jnp.sum(ex, axis=-1, keepdims=True)
        afft = aff.T
        for e in range(n_exp):
            afft_ref[e] = afft[e:e + 1, :]
            colb_ref[e] = jnp.broadcast_to(aff[:, e:e + 1], (n, LANES))
        si = lax.broadcasted_iota(jnp.int32, (LANES, LANES), 0)
        ti = lax.broadcasted_iota(jnp.int32, (LANES, LANES), 1)
        earlier = si < ti
        slot = lax.broadcasted_iota(jnp.int32, (cap, n), 0).astype(F32)

        def per_expert(e, carry):
            parts = []
            for tt in range(nt):
                row = afft_ref[e, :, tt * LANES:(tt + 1) * LANES]
                acc = jnp.zeros((LANES, LANES), F32)
                for c in range(nt):
                    col = colb_ref[e, c * LANES:(c + 1) * LANES, :]
                    if c < tt:
                        ahead = jnp.where(col >= row, 1.0, 0.0)
                    elif c > tt:
                        ahead = jnp.where(col > row, 1.0, 0.0)
                    else:
                        ahead = jnp.where(earlier, jnp.where(col >= row, 1.0, 0.0), jnp.where(col > row, 1.0, 0.0))
                    acc = acc + ahead
                parts.append(jnp.sum(acc, axis=0, keepdims=True))
            rank = jnp.concatenate(parts, axis=1)
            sel = rank == slot
            pg_ref[pl.ds(pl.multiple_of(e * cap, cap), cap), :] = jnp.where(sel, 1.0, 0.0).astype(BF16)
            g_ref[e] = jnp.sum(jnp.where(sel, afft_ref[e], 0.0), axis=1, keepdims=True)
            return carry

        lax.fori_loop(0, n_exp, per_expert, 0, unroll=4 if nt <= 2 else 1)

    xs = jnp.dot(pg_ref[...], h_ref[...], preferred_element_type=F32)
    for e in range(n_exp):
        xs_ref[e] = xs[e * cap:(e + 1) * cap, :].astype(xs_ref.dtype)


def _router(logits, h2, n, nseq, row0, n_exp):
    t, d = h2.shape
    cap = EC_FACTOR * n // n_exp
    td = _seq_col_tile(n, d)
    rb0 = row0 // n
    return pl.pallas_call(
        functools.partial(_router_kernel, n=n, n_exp=n_exp, cap=cap),
        out_shape=[jax.ShapeDtypeStruct((n_exp, nseq * cap, d), BF16),
                   jax.ShapeDtypeStruct((n_exp, nseq * cap, 1), F32),
                   jax.ShapeDtypeStruct((nseq, n_exp * cap, n), BF16)],
        grid=(nseq, d // td),
        in_specs=[pl.BlockSpec((n, LANES), lambda b, j: (rb0 + b, 0)),
                  pl.BlockSpec((n, td), lambda b, j: (rb0 + b, j))],
        out_specs=[pl.BlockSpec((n_exp, cap, td), lambda b, j: (0, b, j)),
                   pl.BlockSpec((n_exp, cap, 1), lambda b, j: (0, b, 0)),
                   pl.BlockSpec((None, n_exp * cap, n), lambda b, j: (b, 0, 0))],
        scratch_shapes=[pltpu.VMEM((n_exp, n, LANES), F32), pltpu.VMEM((n_exp, 1, n), F32)],
        compiler_params=_params(("arbitrary", "arbitrary"), 48), name="router",
    )(logits, h2)


def _scatter_kernel(pg_ref, ye_ref, x_ref, ga_ref, o_ref, *, n_exp):
    ye = jnp.concatenate([ye_ref[e] for e in range(n_exp)], axis=0)
    y = lax.dot_general(pg_ref[...], ye, TN_DIMS, preferred_element_type=F32)
    o_ref[...] = x_ref[...] + ga_ref[...] * y


def _scatter(pg, ye, x1_part, modrows, ga_off, n, row0, slot0):
    rows, d = x1_part.shape
    nseq = rows // n
    n_exp = ye.shape[0]
    cap = pg.shape[1] // n_exp
    td = _seq_col_tile(n, d)
    sb0 = slot0 // cap
    mb0 = row0 // MOD_BLK
    mstep = n // MOD_BLK
    gb = ga_off // td
    return pl.pallas_call(
        functools.partial(_scatter_kernel, n_exp=n_exp),
        out_shape=jax.ShapeDtypeStruct((rows, d), F32), grid=(nseq, d // td),
        in_specs=[pl.BlockSpec((None, n_exp * cap, n), lambda b, j: (b, 0, 0)),
                  pl.BlockSpec((n_exp, cap, td), lambda b, j: (0, sb0 + b, j)),
                  pl.BlockSpec((n, td), lambda b, j: (b, j)),
                  pl.BlockSpec((None, 1, td), lambda b, j: (mb0 + b * mstep, 0, gb + j))],
        out_specs=pl.BlockSpec((n, td), lambda b, j: (b, j)),
        compiler_params=_params(("arbitrary", "arbitrary"), 48), name="scatter",
    )(pg, ye, x1_part, modrows)


def _rope_tables(n, hd):
    qtr = hd // 4
    rows = n // GRID_W
    rpos = jnp.repeat(jnp.arange(rows, dtype=jnp.int32), GRID_W)
    cpos = jnp.arange(rows * GRID_W, dtype=jnp.int32) % GRID_W
    inv = ROPE_THETA ** (-jnp.arange(qtr, dtype=F32) / qtr)
    ar = rpos.astype(F32)[:, None] * inv[None, :]
    ac = cpos.astype(F32)[:, None] * inv[None, :]
    cos = jnp.concatenate([jnp.cos(ar), jnp.cos(ar), jnp.cos(ac), jnp.cos(ac)], axis=-1)
    sin = jnp.concatenate([-jnp.sin(ar), jnp.sin(ar), -jnp.sin(ac), jnp.sin(ac)], axis=-1)
    return cos, sin


def kernel(x_prompt, x_sample, cache_k, cache_v, state_gla_fwd, state_gla_bwd, c, c_ctx, w_mod, b_mod, norm1, norm2, w_in, conv_w, q_norm, k_norm, gla_w2, gla_b, gla_norm, w_conv_out, w_att_out, w_gla_out, w_o, w_router, w_gate, w_up, w_down):
    b, s, d = x_prompt.shape
    bd, sd, _ = x_sample.shape
    depth = w_mod.shape[0]
    cw = conv_w.shape[-1]
    hd = q_norm.shape[-1]
    nkv = cache_k.shape[3]
    aw = w_att_out.shape[1]
    groups = aw // hd // nkv
    gh, dk, dv = state_gla_fwd.shape[2:]
    rank = gla_w2.shape[2]
    n_exp = w_router.shape[-1]
    t = b * s + bd * sd
    row_lat = b * s

    n_conv = 3 * cw
    o_aq = 0
    o_ak = o_aq + aw
    o_av = o_ak + nkv * hd
    o_gq = o_av + nkv * hd
    o_gk = o_gq + gh * dk
    o_gv = o_gk + gh * dk
    o_gr = o_gv + gh * dv
    n_mix = o_gr + gh * dv
    o_lr = n_conv + n_mix
    o_m = o_lr + 2 * rank
    assert s % MOD_BLK == 0 and sd % MOD_BLK == 0 and sd % GRID_W == 0 and s % LANES == 0
    assert row_lat % sd == 0 and 2 * rank <= LANES and o_lr + LANES <= w_in.shape[-1]
    dims = dict(t=t, s=s, sd=sd, b=b, bd=bd, nkv=nkv, groups=groups, hd=hd, gh=gh, dk=dk, dv=dv, depth=depth,
                row_lat=row_lat, o_aq=o_aq, o_ak=o_ak, o_av=o_av, o_gq=o_gq, o_gk=o_gk, o_gv=o_gv, o_gr=o_gr,
                o_lr=o_lr)

    cap_c = EC_FACTOR * s // n_exp
    cap_l = EC_FACTOR * sd // n_exp
    mx = b * cap_c + bd * cap_l
    assert (b * cap_c) % cap_l == 0

    cond8 = jnp.zeros((8, d), F32).at[0].set(c_ctx).at[1:1 + bd].set(c)
    mod_all = _modulation(cond8, w_mod, b_mod)
    blk_row = np.concatenate([np.zeros(b * s // MOD_BLK, np.int32),
                              np.repeat(np.arange(1, bd + 1, dtype=np.int32), sd // MOD_BLK)])

    x_ctx = x_prompt.reshape(b * s, d)
    x_lat = x_sample.reshape(bd * sd, d)
    cos, sin = _rope_tables(sd, hd)
    w2p = jnp.zeros((depth, 2, LANES, gh * dk), F32)
    w2p = w2p.at[:, 0, :rank].set(gla_w2[:, 0]).at[:, 1, rank:2 * rank].set(gla_w2[:, 1])
    wr_pad = jnp.zeros((depth, d, LANES), F32).at[:, :, :n_exp].set(w_router)
    q_norm3 = q_norm.reshape(depth, 1, hd)
    k_norm3 = k_norm.reshape(depth, 1, hd)
    gla_norm3 = gla_norm.reshape(depth, 1, dv)
    cache_k4 = cache_k.reshape(bd, depth, cache_k.shape[2], nkv * hd)
    cache_v4 = cache_v.reshape(bd, depth, cache_v.shape[2], nkv * hd)
    w_in_t = jnp.swapaxes(w_in, 1, 2)
    w_conv_out_b, w_att_out_b, w_gla_out_b = (w.astype(BF16) for w in (w_conv_out, w_att_out, w_gla_out))

    new_k, new_v = [], []
    sf = sb = None
    for l in range(depth):
        modrows = mod_all[l][blk_row][:, None, :]
        (h,) = _normmod(x_ctx, x_lat, modrows, norm1[l], 0, d)
        p_conv = _proj(h, w_in_t, l, 0, n_conv, BF16, _epi_identity, "conv_proj")
        p = _proj(h, w_in_t, l, n_conv, n_mix, F32, _epi_identity, "in_proj")
        plr = _proj(h, w_in_t, l, o_lr, LANES, F32, _epi_identity, "lr_proj", tn=LANES)

        y_conv = _conv(p_conv, conv_w, l, t, cw, s, sd, row_lat)

        y_att_c, k_new = _attn_ctx(p, q_norm3, k_norm3, l, dims)
        y_att_l = _attn_lat(p, q_norm3, k_norm3, cos, sin, cache_k4, cache_v4, l, dims)
        new_k.append(k_new.reshape(b, s, nkv, hd))
        new_v.append(p[:row_lat, o_av:o_av + nkv * hd].reshape(b, s, nkv, hd))

        y_gla_c, sf, sb = _gla_ctx(p, plr, w2p, gla_b, gla_norm3, sf, sb, l, dims)
        y_gla_l = _gla_lat(p, plr, w2p, gla_b, gla_norm3, state_gla_fwd, state_gla_bwd, l, dims)

        gates = _proj(h, w_in_t, l, o_m, 3 * d, BF16, _epi_sigmoid, "gate_proj")
        out_w = (w_conv_out_b, w_att_out_b, w_gla_out_b)
        merged_c = _merge(y_conv, y_att_c, y_gla_c, gates, *out_w, l, d, 0)
        merged_l = _merge(y_conv, y_att_l, y_gla_l, gates, *out_w, l, d, row_lat)
        x1_ctx = _out_proj(merged_c, w_o, l, x_ctx, 0, modrows, 2 * d)
        x1_lat = _out_proj(merged_l, w_o, l, x_lat, row_lat, modrows, 2 * d)

        h2, logits = _normmod(x1_ctx, x1_lat, modrows, norm2[l], 3 * d, 4 * d, w_router_pad=wr_pad[l])
        xs_c, g_c, pg_c = _router(logits, h2, s, b, 0, n_exp)
        xs_l, g_l, pg_l = _router(logits, h2, sd, bd, row_lat, n_exp)
        hid = _ffn_up((xs_c, xs_l), w_gate, w_up, l)
        ye = _ffn_down(hid, w_down, (g_c, g_l), l)
        x_ctx = _scatter(pg_c, ye, x1_ctx, modrows, 5 * d, s, 0, 0)
        x_lat = _scatter(pg_l, ye, x1_lat, modrows, 5 * d, sd, row_lat, b * cap_c)

    y_prompt = x_ctx.reshape(b, s, d)
    y_sample = x_lat.reshape(bd, sd, d)
    return (y_prompt, y_sample, jnp.stack(new_k, axis=1), jnp.stack(new_v, axis=1), sf, sb)
```
